```python
import math
import jax, jax.numpy as jnp
from jax import lax
import numpy as np

D_MODEL = 1024
BATCH = 16
SEQ = 2048
DEPTH = 4

HEAD_DIM = 64
RMS_EPS = 1e-6
RWKV_HEADS = 4
RWKV_WIDTH = RWKV_HEADS * HEAD_DIM
RWKV_DECAY_RANK = 64
RWKV_ICLR_RANK = 64
RWKV_GATE_RANK = 128
RWKV_GN_EPS = 64e-5
GDN_HEADS = 4
GDN_WIDTH = GDN_HEADS * HEAD_DIM
GDN_CONV = 4
GDN_CHUNK = 64
ATTN_Q_HEADS = 8
ATTN_KV_HEADS = 2
ATTN_WIDTH = ATTN_Q_HEADS * HEAD_DIM
ATTN_KV_WIDTH = ATTN_KV_HEADS * HEAD_DIM
WINDOW = 128
ATTN_BLOCK = 128
NUM_BUCKETS = 32
MAX_DISTANCE = 128
D_MIX = RWKV_WIDTH + GDN_WIDTH + ATTN_WIDTH
D_FF = 4 * D_MODEL
RWKV_IN = 3 * RWKV_WIDTH + RWKV_DECAY_RANK + RWKV_ICLR_RANK + RWKV_GATE_RANK
GDN_IN = 4 * GDN_WIDTH + 2 * GDN_HEADS
ATTN_IN = ATTN_WIDTH + 2 * ATTN_KV_WIDTH
D_IN = RWKV_IN + GDN_IN + ATTN_IN

kernel_name = "hymba_style_rwkv7_gdn_swa_hybrid"


def split_cols(z, sizes):
    offs = [int(o) for o in np.cumsum(sizes)[:-1]]
    return jnp.split(z, offs, axis=-1)


def rms_norm(x, w, eps=RMS_EPS):
    xf = x.astype(jnp.float32)
    y = xf * lax.rsqrt(jnp.mean(xf * xf, axis=-1, keepdims=True) + eps)
    return (y * w).astype(x.dtype)


def l2_normalize(x, eps=1e-6):
    return x * lax.rsqrt(jnp.sum(x * x, axis=-1, keepdims=True) + eps)


def token_shift(z):
    return jnp.pad(z, ((0, 0), (1, 0), (0, 0)))[:, :-1]


def causal_depthwise_conv(x, w):
    K, C = w.shape
    return lax.conv_general_dilated(x, w.astype(x.dtype)[:, None, :], window_strides=(1,),
                                    padding=[(K - 1, 0)], dimension_numbers=('NWC', 'WIO', 'NWC'),
                                    feature_group_count=C)


def rwkv7_recurrence(r, w, k, v, a, b):
    Bsz, S, H, N = r.shape

    def step(state, inp):
        r_t, w_t, k_t, v_t, a_t, b_t = inp
        sa = jnp.einsum('bhij,bhj->bhi', state, a_t)
        state = (state * w_t[:, :, None, :] + sa[..., None] * b_t[:, :, None, :]
                 + v_t[..., None] * k_t[:, :, None, :])
        return state, jnp.einsum('bhij,bhj->bhi', state, r_t)

    xs = tuple(jnp.moveaxis(t, 1, 0) for t in (r, w, k, v, a, b))
    state0 = jnp.zeros((Bsz, H, N, N), jnp.float32)
    _, y = lax.scan(step, state0, xs)
    return jnp.moveaxis(y, 0, 1)


def rwkv7_mixer(z, mu, w0, w_up, a0, a_up, g_up, k_k, k_a, r_k, lnx_w, lnx_b):
    Bsz, S, _ = z.shape
    H, N = RWKV_HEADS, HEAD_DIM
    z = z + (token_shift(z) - z) * mu
    r, k, v, dw, da, dg = split_cols(z, (RWKV_WIDTH, RWKV_WIDTH, RWKV_WIDTH,
                                         RWKV_DECAY_RANK, RWKV_ICLR_RANK, RWKV_GATE_RANK))
    log_w = -jax.nn.softplus(-(w0 + jnp.tanh(dw) @ w_up)) - 0.5
    decay = jnp.exp(-jnp.exp(log_w))
    iclr = jax.nn.sigmoid(a0 + da @ a_up)
    gate = jax.nn.sigmoid(dg) @ g_up
    heads = lambda t: t.reshape(Bsz, S, H, N)
    kk = l2_normalize(heads(k * k_k))
    k = k * (1.0 + (iclr - 1.0) * k_a)
    r_h, k_h, v_h = heads(r), heads(k), heads(v)
    y = rwkv7_recurrence(r_h, heads(decay), k_h, v_h, -kk, kk * heads(iclr))
    mean = jnp.mean(y, axis=-1, keepdims=True)
    var = jnp.mean(jnp.square(y - mean), axis=-1, keepdims=True)
    y = ((y - mean) * lax.rsqrt(var + RWKV_GN_EPS)).reshape(Bsz, S, RWKV_WIDTH) * lnx_w + lnx_b
    bonus = jnp.sum(r_h * k_h * r_k, axis=-1, keepdims=True) * v_h
    return (y + bonus.reshape(Bsz, S, RWKV_WIDTH)) * gate


def gated_delta_rule_chunked(q, k, v, g, beta):
    Bsz, S, H, Dk = q.shape
    Dv = v.shape[-1]
    C = GDN_CHUNK
    N = S // C
    to_chunks = lambda t: jnp.moveaxis(t.reshape(Bsz, N, C, H, -1), 3, 1)
    q = to_chunks(q) * (Dk ** -0.5)
    k = to_chunks(k)
    v = to_chunks(v)
    g = jnp.cumsum(to_chunks(g[..., None])[..., 0], axis=-1)
    beta = to_chunks(beta[..., None])
    causal = jnp.tril(jnp.ones((C, C), dtype=bool))
    decay = jnp.exp(jnp.where(causal, g[..., :, None] - g[..., None, :], -jnp.inf))
    eye = jnp.eye(C, dtype=q.dtype)
    k_beta = k * beta
    strict = jnp.einsum('bhncd,bhnsd->bhncs', k_beta, k) * decay * (1.0 - eye)
    T = lax.linalg.triangular_solve(eye + strict, jnp.broadcast_to(eye, strict.shape),
                                    left_side=True, lower=True)
    u = T @ (v * beta)
    w = T @ (k_beta * jnp.exp(g)[..., None])
    intra = jnp.einsum('bhncd,bhnsd->bhncs', q, k) * decay
    q_dec = q * jnp.exp(g)[..., None]
    g_last = g[..., -1]
    k_dec = k * jnp.exp(g_last[..., None] - g)[..., None]

    def step(state, inp):
        q_i, k_i, u_i, w_i, a_i, gl_i = inp
        v_new = u_i - w_i @ state
        o = q_i @ state + a_i @ v_new
        state = state * jnp.exp(gl_i)[..., None, None] + jnp.swapaxes(k_i, -1, -2) @ v_new
        return state, o

    xs = tuple(jnp.moveaxis(t, 2, 0) for t in (q_dec, k_dec, u, w, intra, g_last))
    state0 = jnp.zeros((Bsz, H, Dk, Dv), jnp.float32)
    _, o = lax.scan(step, state0, xs)
    return jnp.transpose(o, (1, 0, 3, 2, 4)).reshape(Bsz, S, H, Dv)


def gdn_mixer(z, conv_w, a_log, dt_bias, norm_w):
    Bsz, S, _ = z.shape
    H, D = GDN_HEADS, HEAD_DIM
    qkv, gate, b, a = split_cols(z, (3 * GDN_WIDTH, GDN_WIDTH, H, H))
    qkv = jax.nn.silu(causal_depthwise_conv(qkv, conv_w))
    q, k, v = (t.reshape(Bsz, S, H, D) for t in jnp.split(qkv, 3, axis=-1))
    beta = jax.nn.sigmoid(b)
    g = -jnp.exp(a_log) * jax.nn.softplus(a + dt_bias)
    o = gated_delta_rule_chunked(l2_normalize(q), l2_normalize(k), v, g, beta)
    o = rms_norm(o, norm_w) * jax.nn.silu(gate.reshape(Bsz, S, H, D))
    return o.reshape(Bsz, S, GDN_WIDTH)


def t5_bucket(dist):
    max_exact = NUM_BUCKETS // 2
    nf = jnp.maximum(dist, max_exact).astype(jnp.float32)
    large = max_exact + (jnp.log(nf / max_exact) / math.log(MAX_DISTANCE / max_exact)
                         * (NUM_BUCKETS - max_exact)).astype(jnp.int32)
    large = jnp.minimum(large, NUM_BUCKETS - 1)
    return jnp.where(dist < max_exact, dist, large)


def band_offsets():
    L = ATTN_BLOCK
    i = jnp.arange(L)[:, None]
    j = jnp.arange(2 * L)[None, :]
    return i + L - j


def t5_band_bias(rel_bias):
    dist = jnp.maximum(band_offsets(), 0)
    return jnp.transpose(rel_bias[t5_bucket(dist)], (2, 0, 1))


def band_mask(nb):
    dist = band_offsets()
    in_window = (dist >= 0) & (dist < WINDOW)
    key_exists = (jnp.arange(nb)[:, None, None] > 0) | (jnp.arange(2 * ATTN_BLOCK)[None, None, :] >= ATTN_BLOCK)
    return in_window & key_exists


def swa_mixer(z, q_norm_w, k_norm_w, sinks, bias_band):
    Bsz, S, _ = z.shape
    Hq, Hkv, D, L = ATTN_Q_HEADS, ATTN_KV_HEADS, HEAD_DIM, ATTN_BLOCK
    G = Hq // Hkv
    nb = S // L
    q, k, v = split_cols(z, (ATTN_WIDTH, ATTN_KV_WIDTH, ATTN_KV_WIDTH))
    q = rms_norm(q.reshape(Bsz, S, Hq, D), q_norm_w).reshape(Bsz, nb, L, Hkv, G, D)
    k = rms_norm(k.reshape(Bsz, S, Hkv, D), k_norm_w)
    v = v.reshape(Bsz, S, Hkv, D)

    def band(t):
        tb = jnp.pad(t, ((0, 0), (L, 0), (0, 0), (0, 0))).reshape(Bsz, nb + 1, L, Hkv, D)
        return jnp.concatenate([tb[:, :-1], tb[:, 1:]], axis=2)

    scores = (jnp.einsum('bnqhgd,bnkhd->bhgnqk', q, band(k)) * (D ** -0.5)
              + bias_band.reshape(Hkv, G, 1, L, 2 * L).astype(jnp.float32))
    scores = jnp.where(band_mask(nb), scores, -jnp.inf)
    sink = jnp.broadcast_to(sinks.astype(jnp.float32).reshape(1, Hkv, G, 1, 1, 1), scores.shape[:-1] + (1,))
    probs = jax.nn.softmax(jnp.concatenate([scores, sink], axis=-1), axis=-1)[..., :-1]
    out = jnp.einsum('bhgnqk,bnkhd->bnqhgd', probs, band(v))
    return out.reshape(Bsz, S, ATTN_WIDTH)


def setup_inputs(seed: int = 0) -> dict:
    key = jax.random.key(seed)
    ks = iter(jax.random.split(key, 32))
    nrm = lambda shape, scale: scale * jax.random.normal(next(ks), shape, jnp.float32)
    L = DEPTH
    dt = jnp.exp(jax.random.uniform(next(ks), (L, GDN_HEADS), jnp.float32,
                                    minval=math.log(1e-3), maxval=math.log(1e-1)))
    return {
        "x": nrm((BATCH, SEQ, D_MODEL), 1.0),
        "ln1_w": 1.0 + nrm((L, D_MODEL), 0.1),
        "w_in": nrm((L, D_MODEL, D_IN), D_MODEL ** -0.5),
        "rwkv_mu": jax.random.uniform(next(ks), (L, RWKV_IN), jnp.float32),
        "rwkv_w0": jax.random.uniform(next(ks), (L, RWKV_WIDTH), jnp.float32, minval=-6.0, maxval=-1.0),
        "rwkv_w_up": nrm((L, RWKV_DECAY_RANK, RWKV_WIDTH), 0.1),
        "rwkv_a0": nrm((L, RWKV_WIDTH), 0.1),
        "rwkv_a_up": nrm((L, RWKV_ICLR_RANK, RWKV_WIDTH), RWKV_ICLR_RANK ** -0.5),
        "rwkv_g_up": nrm((L, RWKV_GATE_RANK, RWKV_WIDTH), RWKV_GATE_RANK ** -0.5),
        "rwkv_k_k": 0.85 + nrm((L, RWKV_WIDTH), 0.05),
        "rwkv_k_a": 1.0 + nrm((L, RWKV_WIDTH), 0.05),
        "rwkv_r_k": nrm((L, RWKV_HEADS, HEAD_DIM), 0.1),
        "rwkv_lnx_w": 1.0 + nrm((L, RWKV_WIDTH), 0.1),
        "rwkv_lnx_b": nrm((L, RWKV_WIDTH), 0.02),
        "gdn_conv_w": nrm((L, GDN_CONV, 3 * GDN_WIDTH), GDN_CONV ** -0.5),
        "gdn_a_log": jnp.log(jax.random.uniform(next(ks), (L, GDN_HEADS), jnp.float32, minval=1.0, maxval=16.0)),
        "gdn_dt_bias": dt + jnp.log(-jnp.expm1(-dt)),
        "gdn_norm_w": 1.0 + nrm((L, HEAD_DIM), 0.1),
        "attn_q_norm_w": 1.0 + nrm((L, HEAD_DIM), 0.1),
        "attn_k_norm_w": 1.0 + nrm((L, HEAD_DIM), 0.1),
        "attn_sinks": nrm((L, ATTN_Q_HEADS), 1.0),
        "rel_bias": nrm((NUM_BUCKETS, ATTN_Q_HEADS), 0.5),
        "w_out": nrm((L, D_MIX, D_MODEL), D_MIX ** -0.5),
        "ln2_w": 1.0 + nrm((L, D_MODEL), 0.1),
        "w_ff1": nrm((L, D_MODEL, D_FF), D_MODEL ** -0.5),
        "w_ff2": nrm((L, D_FF, D_MODEL), 0.5 * D_FF ** -0.5),
    }


def reference(x, ln1_w, w_in, rwkv_mu, rwkv_w0, rwkv_w_up, rwkv_a0, rwkv_a_up, rwkv_g_up,
              rwkv_k_k, rwkv_k_a, rwkv_r_k, rwkv_lnx_w, rwkv_lnx_b, gdn_conv_w, gdn_a_log,
              gdn_dt_bias, gdn_norm_w, attn_q_norm_w, attn_k_norm_w, attn_sinks, rel_bias,
              w_out, ln2_w, w_ff1, w_ff2):
    bias_band = t5_band_bias(rel_bias)
    for l in range(DEPTH):
        h = rms_norm(x, ln1_w[l])
        z = (h @ w_in[l]).astype(jnp.float32)
        z_a, z_b, z_c = split_cols(z, (RWKV_IN, GDN_IN, ATTN_IN))
        y_a = rwkv7_mixer(z_a, rwkv_mu[l], rwkv_w0[l], rwkv_w_up[l], rwkv_a0[l], rwkv_a_up[l],
                          rwkv_g_up[l], rwkv_k_k[l], rwkv_k_a[l], rwkv_r_k[l], rwkv_lnx_w[l], rwkv_lnx_b[l])
        y_b = gdn_mixer(z_b, gdn_conv_w[l], gdn_a_log[l], gdn_dt_bias[l], gdn_norm_w[l])
        y_c = swa_mixer(z_c, attn_q_norm_w[l], attn_k_norm_w[l], attn_sinks[l], bias_band)
        y = jnp.concatenate([y_a, y_b, y_c], axis=-1).astype(x.dtype)
        x = x + y @ w_out[l]
        h = rms_norm(x, ln2_w[l])
        x = x + jnp.square(jax.nn.relu(h @ w_ff1[l])) @ w_ff2[l]
    return x
```

```python
import functools
import math

import numpy as np
import jax
import jax.numpy as jnp
from jax import lax
from jax.experimental import pallas as pl
from jax.experimental.pallas import tpu as pltpu

F32 = jnp.float32
BF16 = jnp.bfloat16

D_MODEL = 1024
DEPTH = 4
HEAD_DIM = 64
RMS_EPS = 1e-6
L2_EPS = 1e-6
N_HEADS = 4
MIX_W = N_HEADS * HEAD_DIM
RWKV_DECAY_RANK = 64
RWKV_ICLR_RANK = 64
RWKV_GATE_RANK = 128
RWKV_GN_EPS = 64e-5
RWKV_IN = 3 * MIX_W + RWKV_DECAY_RANK + RWKV_ICLR_RANK + RWKV_GATE_RANK
GDN_CONV = 4
GDN_MAIN = 4 * MIX_W
GDN_IN = GDN_MAIN + 2 * N_HEADS
ATTN_Q_HEADS = 8
ATTN_KV_HEADS = 2
ATTN_W = ATTN_Q_HEADS * HEAD_DIM
ATTN_KV_W = ATTN_KV_HEADS * HEAD_DIM
ATTN_IN = ATTN_W + 2 * ATTN_KV_W
ATTN_BLOCK = 128
WINDOW = 128
NUM_BUCKETS = 32
MAX_DISTANCE = 128
D_FF = 4 * D_MODEL
LANE = 128
CHUNK = 64
MIX_TB = 256
VMEM_LIMIT = 56 * 1024 * 1024


def _dot(a, b):
    return jnp.dot(a.astype(BF16), b.astype(BF16), preferred_element_type=F32)


def _dot_hi(a, b):
    return jnp.dot(a, b, precision=lax.Precision.HIGHEST, preferred_element_type=F32)


def _dot_nt(a, b, hi=False):
    dn = (((1,), (1,)), ((), ()))
    if hi:
        return lax.dot_general(a, b, dn, precision=lax.Precision.HIGHEST, preferred_element_type=F32)
    return lax.dot_general(a.astype(BF16), b.astype(BF16), dn, preferred_element_type=F32)


def _dot_tn(a, b):
    return jnp.dot(a.T.astype(BF16), b.astype(BF16), preferred_element_type=F32)


def _split_dot_r(x, m_bf16, parts):
    acc = None
    r = x
    for p in range(parts):
        hi = r.astype(BF16)
        t = jnp.dot(hi, m_bf16, preferred_element_type=F32)
        acc = t if acc is None else acc + t
        if p + 1 < parts:
            r = r - hi.astype(F32)
    return acc


def _split_dot_l(m_bf16, x, parts):
    acc = None
    r = x
    for p in range(parts):
        hi = r.astype(BF16)
        t = jnp.dot(m_bf16, hi, preferred_element_type=F32)
        acc = t if acc is None else acc + t
        if p + 1 < parts:
            r = r - hi.astype(F32)
    return acc


def _sigmoid(x):
    return 1.0 / (1.0 + jnp.exp(-x))


def _softplus(x):
    return jnp.maximum(x, 0.0) + jnp.log(1.0 + jnp.exp(-jnp.abs(x)))


def _iota2(shape, dim):
    return lax.broadcasted_iota(jnp.int32, shape, dim)


def _head_ones(n):
    r = _iota2((n, n), 0) // HEAD_DIM
    c = _iota2((n, n), 1) // HEAD_DIM
    return (r == c).astype(BF16)


def _chunk_masks(tb):
    r = _iota2((tb, tb), 0)
    c = _iota2((tb, tb), 1)
    same = (r // CHUNK) == (c // CHUNK)
    return same & (c < r), same & (c <= r)


def _tri_inv(a):
    n = a.shape[0]
    eye = (_iota2((n, n), 0) == _iota2((n, n), 1)).astype(F32)
    p = eye + a
    pw = a
    steps = int(math.log2(CHUNK)) - 1
    for _ in range(steps):
        pw = _dot_hi(pw, pw)
        p = p + _dot_hi(p, pw)
    return p


def _chain(qp, y0, w2, u0, bh, kh_v, decay_rows, st_ref, tb):
    n = st_ref.shape[0]
    bmask = (_iota2((n, n), 0) // HEAD_DIM) == (_iota2((n, n), 1) // HEAD_DIM)
    ht = st_ref[...]
    ys = []
    for c in range(tb // CHUNK):
        sl = slice(c * CHUNK, (c + 1) * CHUNK)
        ys.append(_dot_nt(qp[sl], ht) + y0[sl])
        mt = jnp.where(bmask, _dot_tn(w2[sl], bh[sl]), 0.0)
        if kh_v is None:
            nt = _dot_tn(u0[sl], bh[sl])
        else:
            kh, v = kh_v
            nt = _dot_tn(jnp.concatenate([u0[sl], v[sl]], axis=0),
                         jnp.concatenate([bh[sl], kh[sl]], axis=0))
        nt = jnp.where(bmask, nt, 0.0)
        ht = ht * decay_rows[c] + _dot(ht, mt) + nt
    st_ref[...] = ht
    return jnp.concatenate(ys, axis=0)


def _cumsum_chunks(x, incl_mask):
    return _split_dot_l(incl_mask.astype(BF16), x, 3)


def _inproj_kernel(x_ref, lnw_ref, w_ref, za_ref, zb_ref, zc_ref, zba_ref):
    x = x_ref[...]
    h = x * lax.rsqrt(jnp.mean(x * x, axis=-1, keepdims=True) + RMS_EPS) * lnw_ref[...]
    hb = h.astype(BF16)
    o0 = RWKV_IN
    o1 = o0 + GDN_MAIN
    o2 = o1 + ATTN_IN
    za_ref[...] = jnp.dot(hb, w_ref[:, 0:o0], preferred_element_type=F32)
    zb_ref[...] = jnp.dot(hb, w_ref[:, o0:o1], preferred_element_type=F32)
    zc_ref[...] = jnp.dot(hb, w_ref[:, o1:o2], preferred_element_type=F32)
    zba_ref[...] = jnp.dot(hb, w_ref[:, o2:o2 + LANE], preferred_element_type=F32)


def _inproj(x2, lnw, w, tm):
    t = x2.shape[0]
    n = w.shape[1]
    row = lambda i: (i, 0)
    fixed = lambda i: (0, 0)
    return pl.pallas_call(
        _inproj_kernel,
        grid=(t // tm,),
        in_specs=[pl.BlockSpec((tm, D_MODEL), row),
                  pl.BlockSpec((1, D_MODEL), fixed),
                  pl.BlockSpec((D_MODEL, n), fixed)],
        out_specs=[pl.BlockSpec((tm, RWKV_IN), row),
                   pl.BlockSpec((tm, GDN_MAIN), row),
                   pl.BlockSpec((tm, ATTN_IN), row),
                   pl.BlockSpec((tm, LANE), row)],
        out_shape=[jax.ShapeDtypeStruct((t, RWKV_IN), F32),
                   jax.ShapeDtypeStruct((t, GDN_MAIN), F32),
                   jax.ShapeDtypeStruct((t, ATTN_IN), F32),
                   jax.ShapeDtypeStruct((t, LANE), F32)],
        compiler_params=pltpu.CompilerParams(dimension_semantics=("parallel",),
                                             vmem_limit_bytes=VMEM_LIMIT),
        name="inproj",
    )(x2, lnw, w)


def _rwkv_kernel(z_ref, mu_ref, w0_ref, wup_ref, a0_ref, aup_ref, gup_ref, kk_ref, ka_ref, rk_ref,
                 lnw_ref, lnb_ref, o_ref, prev_ref, st_ref, *, tb):
    @pl.when(pl.program_id(1) == 0)
    def _():
        prev_ref[...] = jnp.zeros_like(prev_ref)
        st_ref[...] = jnp.zeros_like(st_ref)

    w = MIX_W
    z = z_ref[0]
    rolled = pltpu.roll(z, 1, axis=0)
    zprev = jnp.where(_iota2(z.shape, 0) == 0, prev_ref[0:1, :], rolled)
    prev_ref[0:1, :] = z[tb - 1:tb, :]
    zz = z + (zprev - z) * mu_ref[...]
    r = zz[:, 0:w]
    k = zz[:, w:2 * w]
    v = zz[:, 2 * w:3 * w]
    dwa = zz[:, 3 * w:3 * w + LANE]
    dg = zz[:, 3 * w + LANE:]
    log_w = -_softplus(-(w0_ref[...] + _dot(jnp.tanh(dwa), wup_ref[...]))) - 0.5
    ld = -jnp.exp(log_w)
    iclr = _sigmoid(a0_ref[...] + _dot(dwa, aup_ref[...]))
    gate = _dot(_sigmoid(dg), gup_ref[...])

    ones = _head_ones(w)
    kkv = k * kk_ref[...]
    kkn = kkv * lax.rsqrt(_split_dot_r(kkv * kkv, ones, 2) + L2_EPS)
    k2 = k * (1.0 + (iclr - 1.0) * ka_ref[...])
    a = -kkn
    b = kkn * iclr

    strict, incl = _chunk_masks(tb)
    cum = _cumsum_chunks(ld, incl)
    e_neg = jnp.exp(-cum)
    at = a * jnp.exp(cum - ld)
    bt = b * e_neg
    kt = k2 * e_neg
    rt = r * jnp.exp(cum)

    lane_head = _iota2((1, w), 1) // HEAD_DIM
    rhs = jnp.concatenate([bt, kt], axis=0)
    w2 = jnp.zeros((tb, w), F32)
    u0 = jnp.zeros((tb, w), F32)
    qp = jnp.zeros((tb, w), F32)
    y0 = jnp.zeros((tb, w), F32)
    for h in range(N_HEADS):
        mh = (lane_head == h).astype(F32)
        at_h = at * mh
        rt_h = rt * mh
        v_h = v * mh
        g = _dot_nt(jnp.concatenate([at_h, rt_h], axis=0), rhs, hi=True)
        a_ab = jnp.where(strict, g[:tb, :tb], 0.0)
        a_ak = jnp.where(strict, g[:tb, tb:], 0.0)
        a_rb = jnp.where(incl, g[tb:, :tb], 0.0)
        a_rk = jnp.where(incl, g[tb:, tb:], 0.0)
        t_inv = _tri_inv(a_ab)
        p = _dot(a_ak, v_h)
        wu = _dot(t_inv, jnp.concatenate([at_h, p], axis=1))
        qy = _dot(a_rb, wu)
        w2 = w2 + wu[:, :w]
        u0 = u0 + wu[:, w:]
        qp = qp + rt_h + qy[:, :w]
        y0 = y0 + qy[:, w:] + _dot(a_rk, v_h)

    nchunk = tb // CHUNK
    lasts = [cum[(c + 1) * CHUNK - 1:(c + 1) * CHUNK, :] for c in range(nchunk)]
    to_end = jnp.concatenate(
        [jnp.exp(lasts[c] - cum[c * CHUNK:(c + 1) * CHUNK]) for c in range(nchunk)], axis=0)
    y = _chain(qp, y0, w2, u0, b * to_end, (k2 * to_end, v), [jnp.exp(l) for l in lasts], st_ref, tb)

    inv_n = 1.0 / HEAD_DIM
    mean = _split_dot_r(y, ones, 2) * inv_n
    d = y - mean
    var = _split_dot_r(d * d, ones, 2) * inv_n
    yn = d * lax.rsqrt(var + RWKV_GN_EPS) * lnw_ref[...] + lnb_ref[...]
    bonus = _split_dot_r(r * k2 * rk_ref[...], ones, 2) * v
    o_ref[0] = ((yn + bonus) * gate).astype(o_ref.dtype)


def _rwkv(za, mu, w0, wup, a0, aup, gup, k_k, k_a, r_k, lnw, lnb, tb):
    bsz, s, _ = za.shape
    vec = lambda n: pl.BlockSpec((1, n), lambda b, j: (0, 0))
    mat = lambda m, n: pl.BlockSpec((m, n), lambda b, j: (0, 0))
    return pl.pallas_call(
        functools.partial(_rwkv_kernel, tb=tb),
        grid=(bsz, s // tb),
        in_specs=[pl.BlockSpec((1, tb, RWKV_IN), lambda b, j: (b, j, 0)),
                  vec(RWKV_IN), vec(MIX_W), mat(LANE, MIX_W), vec(MIX_W), mat(LANE, MIX_W),
                  mat(RWKV_GATE_RANK, MIX_W), vec(MIX_W), vec(MIX_W), vec(MIX_W), vec(MIX_W), vec(MIX_W)],
        out_specs=pl.BlockSpec((1, tb, MIX_W), lambda b, j: (b, j, 0)),
        out_shape=jax.ShapeDtypeStruct((bsz, s, MIX_W), BF16),
        scratch_shapes=[pltpu.VMEM((8, RWKV_IN), F32), pltpu.VMEM((MIX_W, MIX_W), F32)],
        compiler_params=pltpu.CompilerParams(dimension_semantics=("parallel", "arbitrary"),
                                             vmem_limit_bytes=VMEM_LIMIT),
        name="rwkv7",
    )(za, mu, w0, wup, a0, aup, gup, k_k, k_a, r_k, lnw, lnb)


def _gdn_kernel(z_ref, ba_ref, cw_ref, alog_ref, dtb_ref, nw_ref, o_ref, prev_ref, st_ref, *, tb):
    @pl.when(pl.program_id(1) == 0)
    def _():
        prev_ref[...] = jnp.zeros_like(prev_ref)
        st_ref[...] = jnp.zeros_like(st_ref)

    w = MIX_W
    z = z_ref[0]
    x = z[:, :3 * w]
    gate = z[:, 3 * w:]
    tail = prev_ref[...]
    prev_ref[...] = x[tb - 8:tb, :]
    acc = x * cw_ref[GDN_CONV - 1:GDN_CONV, :]
    row8 = _iota2((8, 3 * w), 0)
    for s in range(1, GDN_CONV):
        rolled = pltpu.roll(x, s, axis=0)
        head = jnp.where(row8 < s, pltpu.roll(tail, s, axis=0), rolled[:8])
        xs = jnp.concatenate([head, rolled[8:]], axis=0)
        acc = acc + xs * cw_ref[GDN_CONV - 1 - s:GDN_CONV - s, :]
    qkv = acc * _sigmoid(acc)
    q = qkv[:, :w]
    k = qkv[:, w:2 * w]
    v = qkv[:, 2 * w:]
    ones = _head_ones(w)
    qn = q * lax.rsqrt(_split_dot_r(q * q, ones, 2) + L2_EPS) * (HEAD_DIM ** -0.5)
    kn = k * lax.rsqrt(_split_dot_r(k * k, ones, 2) + L2_EPS)

    ba = ba_ref[0]
    er = _iota2((LANE, w), 0)
    ec = _iota2((LANE, w), 1) // HEAD_DIM
    b_full = _split_dot_r(ba, (er == ec).astype(BF16), 3)
    a_full = _split_dot_r(ba, (er == ec + N_HEADS).astype(BF16), 3)
    beta = _sigmoid(b_full)
    g = -jnp.exp(alog_ref[...]) * _softplus(a_full + dtb_ref[...])

    strict, incl = _chunk_masks(tb)
    gc = _cumsum_chunks(g, incl)
    gt = gc.T
    eg = jnp.exp(gc)
    kb = kn * beta
    vb = v * beta
    qe = qn * eg
    kbe = kb * eg

    lane_head = _iota2((1, w), 1) // HEAD_DIM
    w2 = jnp.zeros((tb, w), F32)
    u0 = jnp.zeros((tb, w), F32)
    qp = jnp.zeros((tb, w), F32)
    y0 = jnp.zeros((tb, w), F32)
    for h in range(N_HEADS):
        mh = (lane_head == h).astype(F32)
        gcol = gc[:, h * HEAD_DIM:h * HEAD_DIM + 1]
        grow = gt[h * HEAD_DIM:h * HEAD_DIM + 1, :]
        dec = jnp.exp(jnp.where(incl, gcol - grow, -jnp.inf))
        s1 = _dot_nt(jnp.concatenate([kb * mh, qn * mh], axis=0), kn, hi=True)
        lower = jnp.where(strict, s1[:tb] * dec, 0.0)
        intra = s1[tb:] * dec
        t_inv = _tri_inv(-lower)
        wu = _dot(t_inv, jnp.concatenate([kbe * mh, vb * mh], axis=1))
        qy = _dot(intra, wu)
        w2 = w2 - wu[:, :w]
        u0 = u0 + wu[:, w:]
        qp = qp + qe * mh - qy[:, :w]
        y0 = y0 + qy[:, w:]

    nchunk = tb // CHUNK
    lasts = [gc[(c + 1) * CHUNK - 1:(c + 1) * CHUNK, :] for c in range(nchunk)]
    kdec = jnp.concatenate(
        [kn[c * CHUNK:(c + 1) * CHUNK] * jnp.exp(lasts[c] - gc[c * CHUNK:(c + 1) * CHUNK])
         for c in range(nchunk)], axis=0)
    o = _chain(qp, y0, w2, u0, kdec, None, [jnp.exp(l) for l in lasts], st_ref, tb)

    ms = _split_dot_r(o * o, ones, 2) * (1.0 / HEAD_DIM)
    on = o * lax.rsqrt(ms + RMS_EPS) * nw_ref[...]
    o_ref[0] = (on * (gate * _sigmoid(gate))).astype(o_ref.dtype)


def _gdn(zb, zba, cw, alog, dtb, nw, tb):
    bsz, s, _ = zb.shape
    vec = lambda n: pl.BlockSpec((1, n), lambda b, j: (0, 0))
    return pl.pallas_call(
        functools.partial(_gdn_kernel, tb=tb),
        grid=(bsz, s // tb),
        in_specs=[pl.BlockSpec((1, tb, GDN_MAIN), lambda b, j: (b, j, 0)),
                  pl.BlockSpec((1, tb, LANE), lambda b, j: (b, j, 0)),
                  pl.BlockSpec((GDN_CONV, 3 * MIX_W), lambda b, j: (0, 0)),
                  vec(MIX_W), vec(MIX_W), vec(MIX_W)],
        out_specs=pl.BlockSpec((1, tb, MIX_W), lambda b, j: (b, j, 0)),
        out_shape=jax.ShapeDtypeStruct((bsz, s, MIX_W), BF16),
        scratch_shapes=[pltpu.VMEM((8, 3 * MIX_W), F32), pltpu.VMEM((MIX_W, MIX_W), F32)],
        compiler_params=pltpu.CompilerParams(dimension_semantics=("parallel", "arbitrary"),
                                             vmem_limit_bytes=VMEM_LIMIT),
        name="gdn",
    )(zb, zba, cw, alog, dtb, nw)


def _t5_bucket_table():
    L = ATTN_BLOCK
    i = np.arange(L)[:, None]
    j = np.arange(2 * L)[None, :]
    dist = np.maximum(i + L - j, 0)
    max_exact = NUM_BUCKETS // 2
    nf = np.maximum(dist, max_exact).astype(np.float32)
    large = max_exact + (np.log(nf / max_exact) / math.log(MAX_DISTANCE / max_exact)
                         * (NUM_BUCKETS - max_exact)).astype(np.int32)
    large = np.minimum(large, NUM_BUCKETS - 1)
    return np.where(dist < max_exact, dist, large).astype(np.int32)


def _bias_kernel(bkt_ref, rel_ref, o_ref):
    bkt = bkt_ref[...]
    for h in range(ATTN_Q_HEADS):
        acc = jnp.zeros(bkt.shape, F32)
        for b in range(NUM_BUCKETS):
            acc = jnp.where(bkt == b, rel_ref[b, h], acc)
        o_ref[h] = acc


def _bias_band(rel_bias):
    L = ATTN_BLOCK
    return pl.pallas_call(
        _bias_kernel,
        in_specs=[pl.BlockSpec((L, 2 * L), lambda: (0, 0)),
                  pl.BlockSpec(memory_space=pltpu.SMEM)],
        out_specs=pl.BlockSpec((ATTN_Q_HEADS, L, 2 * L), lambda: (0, 0, 0)),
        out_shape=jax.ShapeDtypeStruct((ATTN_Q_HEADS, L, 2 * L), F32),
        name="t5_bias",
    )(jnp.asarray(_t5_bucket_table()), rel_bias)


def _swa_kernel(z_ref, bias_ref, qnw_ref, knw_ref, sink_ref, o_ref, kprev_ref, vprev_ref):
    n = pl.program_id(1)

    @pl.when(n == 0)
    def _():
        kprev_ref[...] = jnp.zeros_like(kprev_ref)
        vprev_ref[...] = jnp.zeros_like(vprev_ref)

    L = ATTN_BLOCK
    z = z_ref[0]
    k = z[:, ATTN_W:ATTN_W + ATTN_KV_W]
    v = z[:, ATTN_W + ATTN_KV_W:]
    ones = _head_ones(LANE)
    inv_d = 1.0 / HEAD_DIM
    kn = k * lax.rsqrt(_split_dot_r(k * k, ones, 2) * inv_d + RMS_EPS) * knw_ref[...]
    kall = jnp.concatenate([kprev_ref[...], kn], axis=0)
    vall = jnp.concatenate([vprev_ref[...], v], axis=0)
    kprev_ref[...] = kn
    vprev_ref[...] = v
    kroll = pltpu.roll(kall, HEAD_DIM, axis=1)
    vroll = pltpu.roll(vall, HEAD_DIM, axis=1)
    lo_k = _iota2(kall.shape, 1) < HEAD_DIM
    lo_q = _iota2((L, LANE), 1) < HEAD_DIM

    qi = _iota2((L, 2 * L), 0)
    kj = _iota2((L, 2 * L), 1)
    dist = qi + L - kj
    mask = (dist >= 0) & (dist < WINDOW) & ((n > 0) | (kj >= L))

    outs = []
    for p in range(ATTN_Q_HEADS // 2):
        j = (2 * p) // (ATTN_Q_HEADS // ATTN_KV_HEADS)
        kd = jnp.where(lo_k, kall, kroll) if j == 0 else jnp.where(lo_k, kroll, kall)
        vd = jnp.where(lo_k, vall, vroll) if j == 0 else jnp.where(lo_k, vroll, vall)
        qp = z[:, p * LANE:(p + 1) * LANE]
        qn = qp * lax.rsqrt(_split_dot_r(qp * qp, ones, 2) * inv_d + RMS_EPS) * qnw_ref[...]
        halves = []
        for e in range(2):
            hq = 2 * p + e
            qm = jnp.where(lo_q, qn, 0.0) if e == 0 else jnp.where(lo_q, 0.0, qn)
            sc = _dot_nt(qm, kd) * (HEAD_DIM ** -0.5) + bias_ref[hq]
            sink = sink_ref[hq]
            m = jnp.maximum(jnp.max(jnp.where(mask, sc, -1e30), axis=-1, keepdims=True), sink)
            pe = jnp.where(mask, jnp.exp(sc - m), 0.0)
            den = jnp.sum(pe, axis=-1, keepdims=True) + jnp.exp(sink - m)
            halves.append(_dot(pe, vd) / den)
        outs.append(jnp.where(lo_q, halves[0], halves[1]))
    o_ref[0] = jnp.concatenate(outs, axis=1).astype(o_ref.dtype)


def _swa(zc, bias, qnw, knw, sinks):
    bsz, s, _ = zc.shape
    L = ATTN_BLOCK
    return pl.pallas_call(
        _swa_kernel,
        grid=(bsz, s // L),
        in_specs=[pl.BlockSpec((1, L, ATTN_IN), lambda b, n: (b, n, 0)),
                  pl.BlockSpec((ATTN_Q_HEADS, L, 2 * L), lambda b, n: (0, 0, 0)),
                  pl.BlockSpec((1, LANE), lambda b, n: (0, 0)),
                  pl.BlockSpec((1, LANE), lambda b, n: (0, 0)),
                  pl.BlockSpec(memory_space=pltpu.SMEM)],
        out_specs=pl.BlockSpec((1, L, ATTN_W), lambda b, n: (b, n, 0)),
        out_shape=jax.ShapeDtypeStruct((bsz, s, ATTN_W), BF16),
        scratch_shapes=[pltpu.VMEM((L, ATTN_KV_W), F32), pltpu.VMEM((L, ATTN_KV_W), F32)],
        compiler_params=pltpu.CompilerParams(dimension_semantics=("parallel", "arbitrary"),
                                             vmem_limit_bytes=VMEM_LIMIT),
        name="swa",
    )(zc, bias, qnw, knw, sinks)


def _mlp_kernel(x_ref, ya_ref, yb_ref, yc_ref, wo_ref, ln2_ref, w1_ref, w2_ref, o_ref,
                x1_ref, h_ref, acc_ref):
    j = pl.program_id(1)

    @pl.when(j == 0)
    def _():
        y = (jnp.dot(ya_ref[...], wo_ref[0:MIX_W, :], preferred_element_type=F32)
             + jnp.dot(yb_ref[...], wo_ref[MIX_W:2 * MIX_W, :], preferred_element_type=F32)
             + jnp.dot(yc_ref[...], wo_ref[2 * MIX_W:, :], preferred_element_type=F32))
        x1 = x_ref[...] + y
        x1_ref[...] = x1
        h = x1 * lax.rsqrt(jnp.mean(x1 * x1, axis=-1, keepdims=True) + RMS_EPS) * ln2_ref[...]
        h_ref[...] = h.astype(BF16)
        acc_ref[...] = jnp.zeros_like(acc_ref)

    u = jnp.dot(h_ref[...], w1_ref[...], preferred_element_type=F32)
    u = jnp.square(jnp.maximum(u, 0.0))
    acc_ref[...] += jnp.dot(u.astype(BF16), w2_ref[...], preferred_element_type=F32)

    @pl.when(j == pl.num_programs(1) - 1)
    def _():
        o_ref[...] = x1_ref[...] + acc_ref[...]


def _out_mlp(x2, ya, yb, yc, wo, ln2, w1, w2, tm, tf):
    t = x2.shape[0]
    row = lambda i, j: (i, 0)
    fixed = lambda i, j: (0, 0)
    return pl.pallas_call(
        _mlp_kernel,
        grid=(t // tm, D_FF // tf),
        in_specs=[pl.BlockSpec((tm, D_MODEL), row),
                  pl.BlockSpec((tm, MIX_W), row),
                  pl.BlockSpec((tm, MIX_W), row),
                  pl.BlockSpec((tm, ATTN_W), row),
                  pl.BlockSpec((D_MODEL, D_MODEL), fixed),
                  pl.BlockSpec((1, D_MODEL), fixed),
                  pl.BlockSpec((D_MODEL, tf), lambda i, j: (0, j)),
                  pl.BlockSpec((tf, D_MODEL), lambda i, j: (j, 0))],
        out_specs=pl.BlockSpec((tm, D_MODEL), row),
        out_shape=jax.ShapeDtypeStruct((t, D_MODEL), F32),
        scratch_shapes=[pltpu.VMEM((tm, D_MODEL), F32), pltpu.VMEM((tm, D_MODEL), BF16),
                        pltpu.VMEM((tm, D_MODEL), F32)],
        compiler_params=pltpu.CompilerParams(dimension_semantics=("parallel", "arbitrary"),
                                             vmem_limit_bytes=VMEM_LIMIT),
        name="out_mlp",
    )(x2, ya, yb, yc, wo, ln2, w1, w2)


def _pad_rows(m, rows, offset):
    return jnp.zeros((rows, m.shape[1]), m.dtype).at[offset:offset + m.shape[0]].set(m)


def _per_head(p):
    return jnp.repeat(p, HEAD_DIM)[None, :]


def _layer(x2, bsz, s, bias, p):
    t = bsz * s
    w_in = p["w_in"]
    o_b = RWKV_IN
    o_ba = o_b + GDN_MAIN
    o_c = o_b + GDN_IN
    w_re = jnp.concatenate(
        [w_in[:, :o_ba], w_in[:, o_c:], w_in[:, o_ba:o_c],
         jnp.zeros((D_MODEL, LANE - 2 * N_HEADS), w_in.dtype)], axis=1).astype(BF16)
    za, zb, zc, zba = _inproj(x2, p["ln1_w"][None, :], w_re, min(512, t))

    row = lambda a: a[None, :]
    ya = _rwkv(za.reshape(bsz, s, RWKV_IN), row(p["rwkv_mu"]), row(p["rwkv_w0"]),
               _pad_rows(p["rwkv_w_up"], LANE, 0).astype(BF16), row(p["rwkv_a0"]),
               _pad_rows(p["rwkv_a_up"], LANE, RWKV_DECAY_RANK).astype(BF16),
               p["rwkv_g_up"].astype(BF16), row(p["rwkv_k_k"]), row(p["rwkv_k_a"]),
               p["rwkv_r_k"].reshape(1, MIX_W), row(p["rwkv_lnx_w"]), row(p["rwkv_lnx_b"]),
               min(MIX_TB, s))
    yb = _gdn(zb.reshape(bsz, s, GDN_MAIN), zba.reshape(bsz, s, LANE), p["gdn_conv_w"],
              _per_head(p["gdn_a_log"]), _per_head(p["gdn_dt_bias"]),
              jnp.tile(p["gdn_norm_w"], N_HEADS)[None, :], min(MIX_TB, s))
    yc = _swa(zc.reshape(bsz, s, ATTN_IN), bias, jnp.tile(p["attn_q_norm_w"], 2)[None, :],
              jnp.tile(p["attn_k_norm_w"], 2)[None, :], p["attn_sinks"])
    return _out_mlp(x2, ya.reshape(t, MIX_W), yb.reshape(t, MIX_W), yc.reshape(t, ATTN_W),
                    p["w_out"].astype(BF16), p["ln2_w"][None, :], p["w_ff1"].astype(BF16),
                    p["w_ff2"].astype(BF16), min(512, t), min(1024, D_FF))


_LAYER_PARAMS = ("ln1_w", "w_in", "rwkv_mu", "rwkv_w0", "rwkv_w_up", "rwkv_a0", "rwkv_a_up", "rwkv_g_up",
                 "rwkv_k_k", "rwkv_k_a", "rwkv_r_k", "rwkv_lnx_w", "rwkv_lnx_b", "gdn_conv_w", "gdn_a_log",
                 "gdn_dt_bias", "gdn_norm_w", "attn_q_norm_w", "attn_k_norm_w", "attn_sinks",
                 "w_out", "ln2_w", "w_ff1", "w_ff2")


def kernel(x, ln1_w, w_in, rwkv_mu, rwkv_w0, rwkv_w_up, rwkv_a0, rwkv_a_up, rwkv_g_up, rwkv_k_k, rwkv_k_a,
           rwkv_r_k, rwkv_lnx_w, rwkv_lnx_b, gdn_conv_w, gdn_a_log, gdn_dt_bias, gdn_norm_w, attn_q_norm_w,
           attn_k_norm_w, attn_sinks, rel_bias, w_out, ln2_w, w_ff1, w_ff2):
    stacked = dict(ln1_w=ln1_w, w_in=w_in, rwkv_mu=rwkv_mu, rwkv_w0=rwkv_w0, rwkv_w_up=rwkv_w_up,
                   rwkv_a0=rwkv_a0, rwkv_a_up=rwkv_a_up, rwkv_g_up=rwkv_g_up, rwkv_k_k=rwkv_k_k,
                   rwkv_k_a=rwkv_k_a, rwkv_r_k=rwkv_r_k, rwkv_lnx_w=rwkv_lnx_w, rwkv_lnx_b=rwkv_lnx_b,
                   gdn_conv_w=gdn_conv_w, gdn_a_log=gdn_a_log, gdn_dt_bias=gdn_dt_bias, gdn_norm_w=gdn_norm_w,
                   attn_q_norm_w=attn_q_norm_w, attn_k_norm_w=attn_k_norm_w, attn_sinks=attn_sinks,
                   w_out=w_out, ln2_w=ln2_w, w_ff1=w_ff1, w_ff2=w_ff2)
    bsz, s, _ = x.shape
    bias = _bias_band(rel_bias)
    x2 = x.reshape(bsz * s, D_MODEL)
    for l in range(w_in.shape[0]):
        x2 = _layer(x2, bsz, s, bias, {name: stacked[name][l] for name in _LAYER_PARAMS})
    return x2.reshape(bsz, s, D_MODEL)
```

```python
import functools
import math

import numpy as np
import jax
import jax.numpy as jnp
from jax import lax
from jax.experimental import pallas as pl
from jax.experimental.pallas import tpu as pltpu

F32 = jnp.float32
BF16 = jnp.bfloat16

D_MODEL = 1024
DEPTH = 4
HEAD_DIM = 64
RMS_EPS = 1e-6
L2_EPS = 1e-6
N_HEADS = 4
MIX_W = N_HEADS * HEAD_DIM
RWKV_DECAY_RANK = 64
RWKV_ICLR_RANK = 64
RWKV_GATE_RANK = 128
RWKV_GN_EPS = 64e-5
RWKV_IN = 3 * MIX_W + RWKV_DECAY_RANK + RWKV_ICLR_RANK + RWKV_GATE_RANK
GDN_CONV = 4
GDN_MAIN = 4 * MIX_W
GDN_IN = GDN_MAIN + 2 * N_HEADS
ATTN_Q_HEADS = 8
ATTN_KV_HEADS = 2
ATTN_W = ATTN_Q_HEADS * HEAD_DIM
ATTN_KV_W = ATTN_KV_HEADS * HEAD_DIM
ATTN_IN = ATTN_W + 2 * ATTN_KV_W
ATTN_BLOCK = 128
WINDOW = 128
NUM_BUCKETS = 32
MAX_DISTANCE = 128
D_FF = 4 * D_MODEL
LANE = 128
CHUNK = 64
MIX_TB = 256
SWA_TQ = 256
VMEM_LIMIT = 56 * 1024 * 1024


def _dot(a, b):
    return jnp.dot(a.astype(BF16), b.astype(BF16), preferred_element_type=F32)


def _dot_nt(a, b):
    return lax.dot_general(a.astype(BF16), b.astype(BF16), (((1,), (1,)), ((), ())),
                           preferred_element_type=F32)


def _dot_tn(a, b):
    return jnp.dot(a.T.astype(BF16), b.astype(BF16), preferred_element_type=F32)


def _bf16_parts(x, parts):
    out = []
    r = x
    for p in range(parts):
        hi = r.astype(BF16)
        out.append(hi)
        if p + 1 < parts:
            r = r - hi.astype(F32)
    return out


def _split_dot_r(x, m_bf16, parts):
    rows = x.shape[0]
    t = jnp.dot(jnp.concatenate(_bf16_parts(x, parts), axis=0), m_bf16, preferred_element_type=F32)
    acc = t[:rows]
    for p in range(1, parts):
        acc = acc + t[p * rows:(p + 1) * rows]
    return acc


def _split_dot_l(m_bf16, x, parts):
    cols = x.shape[1]
    t = jnp.dot(m_bf16, jnp.concatenate(_bf16_parts(x, parts), axis=1), preferred_element_type=F32)
    acc = t[:, :cols]
    for p in range(1, parts):
        acc = acc + t[:, p * cols:(p + 1) * cols]
    return acc


def _sigmoid(x):
    return 1.0 / (1.0 + jnp.exp(-x))


def _softplus(x):
    return jnp.maximum(x, 0.0) + jnp.log(1.0 + jnp.exp(-jnp.abs(x)))


def _iota2(shape, dim):
    return lax.broadcasted_iota(jnp.int32, shape, dim)


def _head_ones(n):
    r = _iota2((n, n), 0) // HEAD_DIM
    c = _iota2((n, n), 1) // HEAD_DIM
    return (r == c).astype(BF16)


def _stack_heads(x):
    lane_head = _iota2((1, x.shape[1]), 1) // HEAD_DIM
    return jnp.concatenate([jnp.where(lane_head == h, x, 0.0) for h in range(N_HEADS)], axis=0)


def _unstack_heads(z):
    c = z.shape[0] // N_HEADS
    return (z[0:c] + z[c:2 * c]) + (z[2 * c:3 * c] + z[3 * c:4 * c])


def _tri_inv_all(mats):
    n = mats[0].shape[0]
    r = _iota2((n, n), 0)
    c = _iota2((n, n), 1)
    ts = [jnp.where(r == c, 1.0, jnp.where((r // 2) == (c // 2), a, 0.0)) for a in mats]
    s = 2
    while s < CHUNK:
        off = ((r // (2 * s)) == (c // (2 * s))) & ((r // s) % 2 == 1) & ((c // s) % 2 == 0)
        ts = [t + _dot(t, _dot(jnp.where(off, a, 0.0), t)) for t, a in zip(ts, mats)]
        s *= 2
    return ts


def _chain(qp, y0, w2, u0, bh, kh_v, decay_rows, st_ref):
    n = st_ref.shape[0]
    bmask = (_iota2((n, n), 0) // HEAD_DIM) == (_iota2((n, n), 1) // HEAD_DIM)
    mts = []
    nts = []
    for c in range(len(qp)):
        mts.append(jnp.where(bmask, _dot_tn(w2[c], bh[c]), 0.0))
        if kh_v is None:
            nt = _dot_tn(u0[c], bh[c])
        else:
            kh, v = kh_v
            nt = _dot_tn(jnp.concatenate([u0[c], v[c]], axis=0), jnp.concatenate([bh[c], kh[c]], axis=0))
        nts.append(jnp.where(bmask, nt, 0.0))
    ht = st_ref[...]
    ys = []
    for c in range(len(qp)):
        ys.append(_dot_nt(qp[c], ht) + y0[c])
        ht = ht * decay_rows[c] + _dot(ht, mts[c]) + nts[c]
    st_ref[...] = ht
    return jnp.concatenate(ys, axis=0)


def _cumsum_chunks(x):
    tb = x.shape[0]
    r = _iota2((tb, tb), 0)
    c = _iota2((tb, tb), 1)
    incl = ((r // CHUNK) == (c // CHUNK)) & (c <= r)
    return _split_dot_l(incl.astype(BF16), x, 3)


def _tri_masks():
    t = _iota2((CHUNK, MIX_W), 0)
    s = _iota2((CHUNK, MIX_W), 1) % HEAD_DIM
    return s < t, s <= t


def _inproj_kernel(x_ref, lnw_ref, w_ref, za_ref, zb_ref, zc_ref, zba_ref):
    x = x_ref[...]
    h = x * lax.rsqrt(jnp.mean(x * x, axis=-1, keepdims=True) + RMS_EPS) * lnw_ref[...]
    hb = h.astype(BF16)
    o0 = RWKV_IN
    o1 = o0 + GDN_MAIN
    o2 = o1 + ATTN_IN
    za_ref[...] = jnp.dot(hb, w_ref[:, 0:o0], preferred_element_type=F32)
    zb_ref[...] = jnp.dot(hb, w_ref[:, o0:o1], preferred_element_type=F32)
    zc_ref[...] = jnp.dot(hb, w_ref[:, o1:o2], preferred_element_type=F32)
    zba_ref[...] = jnp.dot(hb, w_ref[:, o2:o2 + LANE], preferred_element_type=F32)


def _inproj(x2, lnw, w, tm):
    t = x2.shape[0]
    n = w.shape[1]
    row = lambda i: (i, 0)
    fixed = lambda i: (0, 0)
    return pl.pallas_call(
        _inproj_kernel,
        grid=(t // tm,),
        in_specs=[pl.BlockSpec((tm, D_MODEL), row),
                  pl.BlockSpec((1, D_MODEL), fixed),
                  pl.BlockSpec((D_MODEL, n), fixed)],
        out_specs=[pl.BlockSpec((tm, RWKV_IN), row),
                   pl.BlockSpec((tm, GDN_MAIN), row),
                   pl.BlockSpec((tm, ATTN_IN), row),
                   pl.BlockSpec((tm, LANE), row)],
        out_shape=[jax.ShapeDtypeStruct((t, RWKV_IN), F32),
                   jax.ShapeDtypeStruct((t, GDN_MAIN), F32),
                   jax.ShapeDtypeStruct((t, ATTN_IN), F32),
                   jax.ShapeDtypeStruct((t, LANE), F32)],
        compiler_params=pltpu.CompilerParams(dimension_semantics=("parallel",),
                                             vmem_limit_bytes=VMEM_LIMIT),
        name="inproj",
    )(x2, lnw, w)


def _rwkv_kernel(z_ref, mu_ref, w0_ref, wup_ref, a0_ref, aup_ref, gup_ref, kk_ref, ka_ref, rk_ref,
                 lnw_ref, lnb_ref, o_ref, prev_ref, st_ref, *, tb):
    @pl.when(pl.program_id(1) == 0)
    def _():
        prev_ref[...] = jnp.zeros_like(prev_ref)
        st_ref[...] = jnp.zeros_like(st_ref)

    w = MIX_W
    z = z_ref[0]
    rolled = pltpu.roll(z, 1, axis=0)
    zprev = jnp.where(_iota2(z.shape, 0) == 0, prev_ref[0:1, :], rolled)
    prev_ref[0:1, :] = z[tb - 1:tb, :]
    zz = z + (zprev - z) * mu_ref[...]
    r = zz[:, 0:w]
    k = zz[:, w:2 * w]
    v = zz[:, 2 * w:3 * w]
    dwa = zz[:, 3 * w:3 * w + LANE]
    dg = zz[:, 3 * w + LANE:]
    log_w = -_softplus(-(w0_ref[...] + _dot(jnp.tanh(dwa), wup_ref[...]))) - 0.5
    ld = -jnp.exp(log_w)
    iclr = _sigmoid(a0_ref[...] + _dot(dwa, aup_ref[...]))
    gate = _dot(_sigmoid(dg), gup_ref[...])

    ones = _head_ones(w)
    kkv = k * kk_ref[...]
    k2 = k * (1.0 + (iclr - 1.0) * ka_ref[...])
    sums = _split_dot_r(jnp.concatenate([kkv * kkv, r * k2 * rk_ref[...]], axis=0), ones, 2)
    kkn = kkv * lax.rsqrt(sums[:tb] + L2_EPS)
    bonus = sums[tb:] * v
    a = -kkn
    b = kkn * iclr

    cum = _cumsum_chunks(ld)
    e_neg = jnp.exp(-cum)
    at = a * jnp.exp(cum - ld)
    bt = b * e_neg
    kt = k2 * e_neg
    rt = r * jnp.exp(cum)

    strict, incl = _tri_masks()
    nchunk = tb // CHUNK
    sls = [slice(c * CHUNK, (c + 1) * CHUNK) for c in range(nchunk)]
    grams = [_dot_nt(jnp.concatenate([at[sl], rt[sl]], axis=0),
                     jnp.concatenate([_stack_heads(bt[sl]), _stack_heads(kt[sl])], axis=0)) for sl in sls]
    a_ab = [_stack_heads(jnp.where(strict, g[:CHUNK, :w], 0.0)) for g in grams]
    a_ak = [_stack_heads(jnp.where(strict, g[:CHUNK, w:], 0.0)) for g in grams]
    a_rb = [_stack_heads(jnp.where(incl, g[CHUNK:, :w], 0.0)) for g in grams]
    a_rk = [_stack_heads(jnp.where(incl, g[CHUNK:, w:], 0.0)) for g in grams]
    t_inv = _tri_inv_all(a_ab)
    pv = [_dot(jnp.concatenate([a_ak[c], a_rk[c]], axis=0), _stack_heads(v[sls[c]])) for c in range(nchunk)]
    wu = [_dot(t_inv[c], jnp.concatenate([_stack_heads(at[sls[c]]), pv[c][:w]], axis=1)) for c in range(nchunk)]
    qy = [_dot(a_rb[c], wu[c]) for c in range(nchunk)]
    w2 = [_unstack_heads(x[:, :w]) for x in wu]
    u0 = [_unstack_heads(x[:, w:]) for x in wu]
    qp = [rt[sls[c]] + _unstack_heads(qy[c][:, :w]) for c in range(nchunk)]
    y0 = [_unstack_heads(qy[c][:, w:] + pv[c][w:]) for c in range(nchunk)]

    lasts = [cum[(c + 1) * CHUNK - 1:(c + 1) * CHUNK, :] for c in range(nchunk)]
    to_end = [jnp.exp(lasts[c] - cum[sls[c]]) for c in range(nchunk)]
    y = _chain(qp, y0, w2, u0, [b[sls[c]] * to_end[c] for c in range(nchunk)],
               ([k2[sls[c]] * to_end[c] for c in range(nchunk)], [v[sl] for sl in sls]),
               [jnp.exp(l) for l in lasts], st_ref)

    inv_n = 1.0 / HEAD_DIM
    mean = _split_dot_r(y, ones, 2) * inv_n
    d = y - mean
    var = _split_dot_r(d * d, ones, 2) * inv_n
    yn = d * lax.rsqrt(var + RWKV_GN_EPS) * lnw_ref[...] + lnb_ref[...]
    o_ref[0] = ((yn + bonus) * gate).astype(o_ref.dtype)


def _rwkv(za, mu, w0, wup, a0, aup, gup, k_k, k_a, r_k, lnw, lnb, tb):
    bsz, s, _ = za.shape
    vec = lambda n: pl.BlockSpec((1, n), lambda b, j: (0, 0))
    mat = lambda m, n: pl.BlockSpec((m, n), lambda b, j: (0, 0))
    return pl.pallas_call(
        functools.partial(_rwkv_kernel, tb=tb),
        grid=(bsz, s // tb),
        in_specs=[pl.BlockSpec((1, tb, RWKV_IN), lambda b, j: (b, j, 0)),
                  vec(RWKV_IN), vec(MIX_W), mat(LANE, MIX_W), vec(MIX_W), mat(LANE, MIX_W),
                  mat(RWKV_GATE_RANK, MIX_W), vec(MIX_W), vec(MIX_W), vec(MIX_W), vec(MIX_W), vec(MIX_W)],
        out_specs=pl.BlockSpec((1, tb, MIX_W), lambda b, j: (b, j, 0)),
        out_shape=jax.ShapeDtypeStruct((bsz, s, MIX_W), BF16),
        scratch_shapes=[pltpu.VMEM((8, RWKV_IN), F32), pltpu.VMEM((MIX_W, MIX_W), F32)],
        compiler_params=pltpu.CompilerParams(dimension_semantics=("parallel", "arbitrary"),
                                             vmem_limit_bytes=VMEM_LIMIT),
        name="rwkv7",
    )(za, mu, w0, wup, a0, aup, gup, k_k, k_a, r_k, lnw, lnb)


def _gdn_kernel(z_ref, ba_ref, cw_ref, alog_ref, dtb_ref, nw_ref, o_ref, prev_ref, st_ref, *, tb):
    @pl.when(pl.program_id(1) == 0)
    def _():
        prev_ref[...] = jnp.zeros_like(prev_ref)
        st_ref[...] = jnp.zeros_like(st_ref)

    w = MIX_W
    z = z_ref[0]
    x = z[:, :3 * w]
    gate = z[:, 3 * w:]
    tail = prev_ref[...]
    prev_ref[...] = x[tb - 8:tb, :]
    acc = x * cw_ref[GDN_CONV - 1:GDN_CONV, :]
    row8 = _iota2((8, 3 * w), 0)
    for s in range(1, GDN_CONV):
        rolled = pltpu.roll(x, s, axis=0)
        head = jnp.where(row8 < s, pltpu.roll(tail, s, axis=0), rolled[:8])
        xs = jnp.concatenate([head, rolled[8:]], axis=0)
        acc = acc + xs * cw_ref[GDN_CONV - 1 - s:GDN_CONV - s, :]
    qkv = acc * _sigmoid(acc)
    q = qkv[:, :w]
    k = qkv[:, w:2 * w]
    v = qkv[:, 2 * w:]
    ones = _head_ones(w)
    sums = _split_dot_r(jnp.concatenate([q * q, k * k], axis=0), ones, 2)
    qn = q * lax.rsqrt(sums[:tb] + L2_EPS) * (HEAD_DIM ** -0.5)
    kn = k * lax.rsqrt(sums[tb:] + L2_EPS)

    ba = ba_ref[0]
    er = _iota2((LANE, 2 * w), 0)
    ec = _iota2((LANE, 2 * w), 1)
    spread = (er == (ec % w) // HEAD_DIM + N_HEADS * (ec // w)).astype(BF16)
    ba_full = _split_dot_r(ba, spread, 3)
    beta = _sigmoid(ba_full[:, :w])
    g = -jnp.exp(alog_ref[...]) * _softplus(ba_full[:, w:] + dtb_ref[...])

    gc = _cumsum_chunks(g)
    gt = gc.T
    eg = jnp.exp(gc)
    kb = kn * beta
    vb = v * beta
    qe = qn * eg
    kbe = kb * eg

    strict, incl = _tri_masks()
    nchunk = tb // CHUNK
    sls = [slice(c * CHUNK, (c + 1) * CHUNK) for c in range(nchunk)]
    lower = []
    intra = []
    for c, sl in enumerate(sls):
        s1 = _dot_nt(jnp.concatenate([kb[sl], qn[sl]], axis=0), _stack_heads(kn[sl]))
        grow = jnp.concatenate([gt[h * HEAD_DIM:h * HEAD_DIM + 1, sl] for h in range(N_HEADS)], axis=1)
        dec = jnp.exp(jnp.where(incl, gc[sl] - grow, -jnp.inf))
        lower.append(_stack_heads(jnp.where(strict, s1[:CHUNK] * dec, 0.0)))
        intra.append(_stack_heads(s1[CHUNK:] * dec))
    t_inv = _tri_inv_all([-m for m in lower])
    wu = [_dot(t_inv[c], jnp.concatenate([_stack_heads(kbe[sls[c]]), _stack_heads(vb[sls[c]])], axis=1))
          for c in range(nchunk)]
    qy = [_dot(intra[c], wu[c]) for c in range(nchunk)]
    w2 = [-_unstack_heads(x[:, :w]) for x in wu]
    u0 = [_unstack_heads(x[:, w:]) for x in wu]
    qp = [qe[sls[c]] - _unstack_heads(qy[c][:, :w]) for c in range(nchunk)]
    y0 = [_unstack_heads(qy[c][:, w:]) for c in range(nchunk)]

    lasts = [gc[(c + 1) * CHUNK - 1:(c + 1) * CHUNK, :] for c in range(nchunk)]
    kdec = [kn[sls[c]] * jnp.exp(lasts[c] - gc[sls[c]]) for c in range(nchunk)]
    o = _chain(qp, y0, w2, u0, kdec, None, [jnp.exp(l) for l in lasts], st_ref)

    ms = _split_dot_r(o * o, ones, 2) * (1.0 / HEAD_DIM)
    on = o * lax.rsqrt(ms + RMS_EPS) * nw_ref[...]
    o_ref[0] = (on * (gate * _sigmoid(gate))).astype(o_ref.dtype)


def _gdn(zb, zba, cw, alog, dtb, nw, tb):
    bsz, s, _ = zb.shape
    vec = lambda n: pl.BlockSpec((1, n), lambda b, j: (0, 0))
    return pl.pallas_call(
        functools.partial(_gdn_kernel, tb=tb),
        grid=(bsz, s // tb),
        in_specs=[pl.BlockSpec((1, tb, GDN_MAIN), lambda b, j: (b, j, 0)),
                  pl.BlockSpec((1, tb, LANE), lambda b, j: (b, j, 0)),
                  pl.BlockSpec((GDN_CONV, 3 * MIX_W), lambda b, j: (0, 0)),
                  vec(MIX_W), vec(MIX_W), vec(MIX_W)],
        out_specs=pl.BlockSpec((1, tb, MIX_W), lambda b, j: (b, j, 0)),
        out_shape=jax.ShapeDtypeStruct((bsz, s, MIX_W), BF16),
        scratch_shapes=[pltpu.VMEM((8, 3 * MIX_W), F32), pltpu.VMEM((MIX_W, MIX_W), F32)],
        compiler_params=pltpu.CompilerParams(dimension_semantics=("parallel", "arbitrary"),
                                             vmem_limit_bytes=VMEM_LIMIT),
        name="gdn",
    )(zb, zba, cw, alog, dtb, nw)


def _t5_bucket_table():
    L = ATTN_BLOCK
    i = np.arange(L)[None, :]
    j = np.arange(2 * L)[:, None]
    dist = np.maximum(i + L - j, 0)
    max_exact = NUM_BUCKETS // 2
    nf = np.maximum(dist, max_exact).astype(np.float32)
    large = max_exact + (np.log(nf / max_exact) / math.log(MAX_DISTANCE / max_exact)
                         * (NUM_BUCKETS - max_exact)).astype(np.int32)
    large = np.minimum(large, NUM_BUCKETS - 1)
    return np.where(dist < max_exact, dist, large).astype(np.int32)


def _bias_kernel(bkt_ref, rel_ref, o_ref):
    L = ATTN_BLOCK
    grp = ATTN_Q_HEADS // ATTN_KV_HEADS
    bkt = bkt_ref[...]
    kj = _iota2(bkt.shape, 0)
    dist = _iota2(bkt.shape, 1) + L - kj
    in_window = (dist >= 0) & (dist < WINDOW)
    for h in range(ATTN_Q_HEADS):
        acc = jnp.zeros(bkt.shape, F32)
        for b in range(NUM_BUCKETS):
            acc = jnp.where(bkt == b, rel_ref[b, h], acc)
        cols = slice((h % grp) * L, (h % grp + 1) * L)
        o_ref[1, h // grp, :, cols] = jnp.where(in_window, acc, -jnp.inf)
        o_ref[0, h // grp, :, cols] = jnp.where(in_window & (kj >= L), acc, -jnp.inf)


def _bias_band(rel_bias):
    L = ATTN_BLOCK
    shape = (2, ATTN_KV_HEADS, 2 * L, (ATTN_Q_HEADS // ATTN_KV_HEADS) * L)
    return pl.pallas_call(
        _bias_kernel,
        in_specs=[pl.BlockSpec((2 * L, L), lambda: (0, 0)),
                  pl.BlockSpec(memory_space=pltpu.SMEM)],
        out_specs=pl.BlockSpec(shape, lambda: (0, 0, 0, 0)),
        out_shape=jax.ShapeDtypeStruct(shape, F32),
        name="t5_bias",
    )(jnp.asarray(_t5_bucket_table()), rel_bias)


def _swa_kernel(z_ref, bias0_ref, bias_ref, qnw_ref, knw_ref, sink_ref, o_ref, kprev_ref, vprev_ref, *, nsub):
    @pl.when(pl.program_id(1) == 0)
    def _():
        kprev_ref[...] = jnp.zeros_like(kprev_ref)
        vprev_ref[...] = jnp.zeros_like(vprev_ref)

    L = ATTN_BLOCK
    tq = nsub * L
    grp = ATTN_Q_HEADS // ATTN_KV_HEADS
    ngroups = ATTN_W // LANE
    z = z_ref[0]
    k = z[:, ATTN_W:ATTN_W + ATTN_KV_W]
    v = z[:, ATTN_W + ATTN_KV_W:]
    qg = [z[:, p * LANE:(p + 1) * LANE] for p in range(ngroups)]
    sq = _split_dot_r(jnp.concatenate([x * x for x in qg] + [k * k], axis=0), _head_ones(LANE), 2)
    inv_d = 1.0 / HEAD_DIM
    qscale = qnw_ref[...] * (HEAD_DIM ** -0.5)
    qn = [qg[p] * lax.rsqrt(sq[p * tq:(p + 1) * tq] * inv_d + RMS_EPS) * qscale for p in range(ngroups)]
    kn = k * lax.rsqrt(sq[ngroups * tq:] * inv_d + RMS_EPS) * knw_ref[...]
    kall = jnp.concatenate([kprev_ref[...], kn], axis=0)
    vall = jnp.concatenate([vprev_ref[...], v], axis=0)
    kprev_ref[...] = kn[tq - L:]
    vprev_ref[...] = v[tq - L:]
    kroll = pltpu.roll(kall, HEAD_DIM, axis=1)
    vroll = pltpu.roll(vall, HEAD_DIM, axis=1)
    lo_k = _iota2(kall.shape, 1) < HEAD_DIM
    lo_q = _iota2((L, LANE), 1) < HEAD_DIM
    kd = [jnp.where(lo_k, kall, kroll), jnp.where(lo_k, kroll, kall)]
    vdt = [jnp.where(lo_k, vall, vroll).T, jnp.where(lo_k, vroll, vall).T]

    qlane_head = _iota2((1, grp * L), 1) // L
    for i in range(nsub):
        rows = slice(i * L, (i + 1) * L)
        keys = slice(i * L, (i + 2) * L)
        outs = []
        for j in range(ATTN_KV_HEADS):
            qs = []
            for g in range(grp):
                hq = grp * j + g
                qrow = qn[hq // 2][rows]
                qs.append(jnp.where(lo_q, qrow, 0.0) if hq % 2 == 0 else jnp.where(lo_q, 0.0, qrow))
            bias = bias0_ref[0, j] if i == 0 else bias_ref[0, j]
            sc = _dot_nt(kd[j][keys], jnp.concatenate(qs, axis=0)) + bias
            sink = jnp.zeros((1, grp * L), F32)
            for g in range(grp):
                sink = jnp.where(qlane_head == g, sink_ref[grp * j + g], sink)
            m = jnp.maximum(jnp.max(sc, axis=0, keepdims=True), sink)
            pe = jnp.exp(sc - m)
            den = jnp.sum(pe, axis=0, keepdims=True) + jnp.exp(sink - m)
            o = (_dot(vdt[j][:, keys], pe) / den).T
            for g in range(0, grp, 2):
                outs.append(jnp.where(lo_q, o[g * L:(g + 1) * L], o[(g + 1) * L:(g + 2) * L]))
        o_ref[0, rows, :] = jnp.concatenate(outs, axis=1).astype(o_ref.dtype)


def _swa(zc, bias, qnw, knw, sinks, tq):
    bsz, s, _ = zc.shape
    L = ATTN_BLOCK
    bias_block = (1,) + bias.shape[1:]
    return pl.pallas_call(
        functools.partial(_swa_kernel, nsub=tq // L),
        grid=(bsz, s // tq),
        in_specs=[pl.BlockSpec((1, tq, ATTN_IN), lambda b, n: (b, n, 0)),
                  pl.BlockSpec(bias_block, lambda b, n: (jnp.minimum(n, 1), 0, 0, 0)),
                  pl.BlockSpec(bias_block, lambda b, n: (1, 0, 0, 0)),
                  pl.BlockSpec((1, LANE), lambda b, n: (0, 0)),
                  pl.BlockSpec((1, LANE), lambda b, n: (0, 0)),
                  pl.BlockSpec(memory_space=pltpu.SMEM)],
        out_specs=pl.BlockSpec((1, tq, ATTN_W), lambda b, n: (b, n, 0)),
        out_shape=jax.ShapeDtypeStruct((bsz, s, ATTN_W), BF16),
        scratch_shapes=[pltpu.VMEM((L, ATTN_KV_W), F32), pltpu.VMEM((L, ATTN_KV_W), F32)],
        compiler_params=pltpu.CompilerParams(dimension_semantics=("parallel", "arbitrary"),
                                             vmem_limit_bytes=VMEM_LIMIT),
        name="swa",
    )(zc, bias, bias, qnw, knw, sinks)


def _mlp_kernel(x_ref, ya_ref, yb_ref, yc_ref, wo_ref, ln2_ref, w1_ref, w2_ref, o_ref,
                x1_ref, h_ref, acc_ref):
    j = pl.program_id(1)

    @pl.when(j == 0)
    def _():
        y = (jnp.dot(ya_ref[...], wo_ref[0:MIX_W, :], preferred_element_type=F32)
             + jnp.dot(yb_ref[...], wo_ref[MIX_W:2 * MIX_W, :], preferred_element_type=F32)
             + jnp.dot(yc_ref[...], wo_ref[2 * MIX_W:, :], preferred_element_type=F32))
        x1 = x_ref[...] + y
        x1_ref[...] = x1
        h = x1 * lax.rsqrt(jnp.mean(x1 * x1, axis=-1, keepdims=True) + RMS_EPS) * ln2_ref[...]
        h_ref[...] = h.astype(BF16)
        acc_ref[...] = jnp.zeros_like(acc_ref)

    u = jnp.dot(h_ref[...], w1_ref[...], preferred_element_type=F32)
    u = jnp.square(jnp.maximum(u, 0.0))
    acc_ref[...] += jnp.dot(u.astype(BF16), w2_ref[...], preferred_element_type=F32)

    @pl.when(j == pl.num_programs(1) - 1)
    def _():
        o_ref[...] = x1_ref[...] + acc_ref[...]


def _out_mlp(x2, ya, yb, yc, wo, ln2, w1, w2, tm, tf):
    t = x2.shape[0]
    row = lambda i, j: (i, 0)
    fixed = lambda i, j: (0, 0)
    return pl.pallas_call(
        _mlp_kernel,
        grid=(t // tm, D_FF // tf),
        in_specs=[pl.BlockSpec((tm, D_MODEL), row),
                  pl.BlockSpec((tm, MIX_W), row),
                  pl.BlockSpec((tm, MIX_W), row),
                  pl.BlockSpec((tm, ATTN_W), row),
                  pl.BlockSpec((D_MODEL, D_MODEL), fixed),
                  pl.BlockSpec((1, D_MODEL), fixed),
                  pl.BlockSpec((D_MODEL, tf), lambda i, j: (0, j)),
                  pl.BlockSpec((tf, D_MODEL), lambda i, j: (j, 0))],
        out_specs=pl.BlockSpec((tm, D_MODEL), row),
        out_shape=jax.ShapeDtypeStruct((t, D_MODEL), F32),
        scratch_shapes=[pltpu.VMEM((tm, D_MODEL), F32), pltpu.VMEM((tm, D_MODEL), BF16),
                        pltpu.VMEM((tm, D_MODEL), F32)],
        compiler_params=pltpu.CompilerParams(dimension_semantics=("parallel", "arbitrary"),
                                             vmem_limit_bytes=VMEM_LIMIT),
        name="out_mlp",
    )(x2, ya, yb, yc, wo, ln2, w1, w2)


def _pad_rows(m, rows, offset):
    return jnp.zeros((rows, m.shape[1]), m.dtype).at[offset:offset + m.shape[0]].set(m)


def _per_head(p):
    return jnp.repeat(p, HEAD_DIM)[None, :]


def _layer(x2, bsz, s, bias, p):
    t = bsz * s
    w_in = p["w_in"]
    o_b = RWKV_IN
    o_ba = o_b + GDN_MAIN
    o_c = o_b + GDN_IN
    w_re = jnp.concatenate(
        [w_in[:, :o_ba], w_in[:, o_c:], w_in[:, o_ba:o_c],
         jnp.zeros((D_MODEL, LANE - 2 * N_HEADS), w_in.dtype)], axis=1).astype(BF16)
    za, zb, zc, zba = _inproj(x2, p["ln1_w"][None, :], w_re, min(512, t))

    row = lambda a: a[None, :]
    ya = _rwkv(za.reshape(bsz, s, RWKV_IN), row(p["rwkv_mu"]), row(p["rwkv_w0"]),
               _pad_rows(p["rwkv_w_up"], LANE, 0).astype(BF16), row(p["rwkv_a0"]),
               _pad_rows(p["rwkv_a_up"], LANE, RWKV_DECAY_RANK).astype(BF16),
               p["rwkv_g_up"].astype(BF16), row(p["rwkv_k_k"]), row(p["rwkv_k_a"]),
               p["rwkv_r_k"].reshape(1, MIX_W), row(p["rwkv_lnx_w"]), row(p["rwkv_lnx_b"]),
               min(MIX_TB, s))
    yb = _gdn(zb.reshape(bsz, s, GDN_MAIN), zba.reshape(bsz, s, LANE), p["gdn_conv_w"],
              _per_head(p["gdn_a_log"]), _per_head(p["gdn_dt_bias"]),
              jnp.tile(p["gdn_norm_w"], N_HEADS)[None, :], min(MIX_TB, s))
    yc = _swa(zc.reshape(bsz, s, ATTN_IN), bias, jnp.tile(p["attn_q_norm_w"], 2)[None, :],
              jnp.tile(p["attn_k_norm_w"], 2)[None, :], p["attn_sinks"], min(SWA_TQ, s))
    return _out_mlp(x2, ya.reshape(t, MIX_W), yb.reshape(t, MIX_W), yc.reshape(t, ATTN_W),
                    p["w_out"].astype(BF16), p["ln2_w"][None, :], p["w_ff1"].astype(BF16),
                    p["w_ff2"].astype(BF16), min(512, t), min(1024, D_FF))


_LAYER_PARAMS = ("ln1_w", "w_in", "rwkv_mu", "rwkv_w0", "rwkv_w_up", "rwkv_a0", "rwkv_a_up", "rwkv_g_up",
                 "rwkv_k_k", "rwkv_k_a", "rwkv_r_k", "rwkv_lnx_w", "rwkv_lnx_b", "gdn_conv_w", "gdn_a_log",
                 "gdn_dt_bias", "gdn_norm_w", "attn_q_norm_w", "attn_k_norm_w", "attn_sinks",
                 "w_out", "ln2_w", "w_ff1", "w_ff2")


def kernel(x, ln1_w, w_in, rwkv_mu, rwkv_w0, rwkv_w_up, rwkv_a0, rwkv_a_up, rwkv_g_up, rwkv_k_k, rwkv_k_a,
           rwkv_r_k, rwkv_lnx_w, rwkv_lnx_b, gdn_conv_w, gdn_a_log, gdn_dt_bias, gdn_norm_w, attn_q_norm_w,
           attn_k_norm_w, attn_sinks, rel_bias, w_out, ln2_w, w_ff1, w_ff2):
    stacked = dict(ln1_w=ln1_w, w_in=w_in, rwkv_mu=rwkv_mu, rwkv_w0=rwkv_w0, rwkv_w_up=rwkv_w_up,
                   rwkv_a0=rwkv_a0, rwkv_a_up=rwkv_a_up, rwkv_g_up=rwkv_g_up, rwkv_k_k=rwkv_k_k,
                   rwkv_k_a=rwkv_k_a, rwkv_r_k=rwkv_r_k, rwkv_lnx_w=rwkv_lnx_w, rwkv_lnx_b=rwkv_lnx_b,
                   gdn_conv_w=gdn_conv_w, gdn_a_log=gdn_a_log, gdn_dt_bias=gdn_dt_bias, gdn_norm_w=gdn_norm_w,
                   attn_q_norm_w=attn_q_norm_w, attn_k_norm_w=attn_k_norm_w, attn_sinks=attn_sinks,
                   w_out=w_out, ln2_w=ln2_w, w_ff1=w_ff1, w_ff2=w_ff2)
    bsz, s, _ = x.shape
    bias = _bias_band(rel_bias)
    x2 = x.reshape(bsz * s, D_MODEL)
    for l in range(w_in.shape[0]):
        x2 = _layer(x2, bsz, s, bias, {name: stacked[name][l] for name in _LAYER_PARAMS})
    return x2.reshape(bsz, s, D_MODEL)
```

```python
import collections
import functools
import math

import numpy as np
import jax
import jax.numpy as jnp
from jax import lax
from jax.experimental import pallas as pl
from jax.experimental.pallas import tpu as pltpu

F32 = jnp.float32
BF16 = jnp.bfloat16

D_MODEL = 1024
DEPTH = 4
HEAD_DIM = 64
RMS_EPS = 1e-6
L2_EPS = 1e-6
N_HEADS = 4
MIX_W = N_HEADS * HEAD_DIM
RWKV_DECAY_RANK = 64
RWKV_ICLR_RANK = 64
RWKV_GATE_RANK = 128
RWKV_GN_EPS = 64e-5
RWKV_IN = 3 * MIX_W + RWKV_DECAY_RANK + RWKV_ICLR_RANK + RWKV_GATE_RANK
GDN_CONV = 4
GDN_MAIN = 4 * MIX_W
GDN_IN = GDN_MAIN + 2 * N_HEADS
ATTN_Q_HEADS = 8
ATTN_KV_HEADS = 2
ATTN_W = ATTN_Q_HEADS * HEAD_DIM
ATTN_KV_W = ATTN_KV_HEADS * HEAD_DIM
ATTN_IN = ATTN_W + 2 * ATTN_KV_W
ATTN_BLOCK = 128
WINDOW = 128
NUM_BUCKETS = 32
MAX_DISTANCE = 128
D_FF = 4 * D_MODEL
LANE = 128
CHUNK = 64
MIX_TB = 256
MIX_NSEQ = 2
MIX_LEAD = 1
SWA_TQ = 512
VMEM_LIMIT = 56 * 1024 * 1024


def _dot(a, b):
    return jnp.dot(a.astype(BF16), b.astype(BF16), preferred_element_type=F32)


def _dot_nt(a, b):
    return lax.dot_general(a.astype(BF16), b.astype(BF16), (((1,), (1,)), ((), ())),
                           preferred_element_type=F32)


def _dot_tn(a, b):
    return jnp.dot(a.T.astype(BF16), b.astype(BF16), preferred_element_type=F32)


def _bf16_parts(x, parts):
    out = []
    r = x
    for p in range(parts):
        hi = r.astype(BF16)
        out.append(hi)
        if p + 1 < parts:
            r = r - hi.astype(F32)
    return out


def _split_dot_r(x, m_bf16, parts):
    rows = x.shape[0]
    t = jnp.dot(jnp.concatenate(_bf16_parts(x, parts), axis=0), m_bf16, preferred_element_type=F32)
    acc = t[:rows]
    for p in range(1, parts):
        acc = acc + t[p * rows:(p + 1) * rows]
    return acc


def _split_dot_l(m_bf16, x, parts):
    cols = x.shape[1]
    t = jnp.dot(m_bf16, jnp.concatenate(_bf16_parts(x, parts), axis=1), preferred_element_type=F32)
    acc = t[:, :cols]
    for p in range(1, parts):
        acc = acc + t[:, p * cols:(p + 1) * cols]
    return acc


def _sigmoid(x):
    return 1.0 / (1.0 + jnp.exp(-x))


def _softplus(x):
    return jnp.maximum(x, 0.0) + jnp.log(1.0 + jnp.exp(-jnp.abs(x)))


def _iota2(shape, dim):
    return lax.broadcasted_iota(jnp.int32, shape, dim)


def _head_ones(n):
    r = _iota2((n, n), 0) // HEAD_DIM
    c = _iota2((n, n), 1) // HEAD_DIM
    return (r == c).astype(BF16)


def _tri_masks():
    t = _iota2((CHUNK, MIX_W), 0)
    s = _iota2((CHUNK, MIX_W), 1) % HEAD_DIM
    return s < t, s <= t


def _run_staggered(gens, lead):
    done = [False] * len(gens)
    rnd = 0
    while not all(done):
        for i, g in enumerate(gens):
            if rnd >= i * lead and not done[i]:
                try:
                    next(g)
                except StopIteration:
                    done[i] = True
        rnd += 1


MixConsts = collections.namedtuple("MixConsts", "ho ho32 tri cum")


def _mixer_consts():
    n = MIX_W
    r = np.arange(n)[:, None]
    c = np.arange(n)[None, :]
    ho = (r // HEAD_DIM) == (c // HEAD_DIM)
    tri = [r == c, (r // 2) == (c // 2)]
    s = 2
    while s < CHUNK:
        tri.append(((r // (2 * s)) == (c // (2 * s))) & ((r // s) % 2 == 1) & ((c // s) % 2 == 0))
        s *= 2
    cum = np.arange(CHUNK)[None, :] <= np.arange(CHUNK)[:, None]
    return MixConsts(jnp.asarray(ho, BF16), jnp.asarray(ho, F32), jnp.asarray(np.stack(tri), BF16),
                     jnp.asarray(cum, BF16))


def _const_specs():
    fixed = lambda shape: pl.BlockSpec(shape, lambda b, j: (0,) * len(shape), pipeline_mode=pl.Buffered(1))
    ntri = int(math.log2(CHUNK)) + 1
    return [fixed((MIX_W, MIX_W)), fixed((MIX_W, MIX_W)), fixed((ntri, MIX_W, MIX_W)), fixed((CHUNK, CHUNK))]


def _stack_heads(xb, ho):
    return jnp.concatenate([xb] * N_HEADS, axis=0) * ho


def _unstack_heads(z):
    c = z.shape[0] // N_HEADS
    return (z[0:c] + z[c:2 * c]) + (z[2 * c:3 * c] + z[3 * c:4 * c])


def _tri_inv_stages(a, tri_ref):
    t = tri_ref[0] + a * tri_ref[1]
    for lvl in range(2, tri_ref.shape[0]):
        inner = jnp.dot(a * tri_ref[lvl], t, preferred_element_type=F32).astype(BF16)
        yield None
        t = t + jnp.dot(t, inner, preferred_element_type=F32).astype(BF16)
        yield None
    yield t


def _chain_step(carry, qp, y0, w2, u0, upd_lhs_extra, upd_rhs, decay_row, ho32):
    ht = carry["ht"]
    uy = _dot_nt(jnp.concatenate([w2, qp], axis=0), ht)
    u = u0 + uy[:CHUNK]
    lhs = u if upd_lhs_extra is None else jnp.concatenate([u, upd_lhs_extra], axis=0)
    carry["ht"] = ht * decay_row + _dot_tn(lhs, upd_rhs) * ho32
    return uy[CHUNK:] + y0


def _shifted_rows(z_ref, prev_ref, q, c):
    if c == 0:
        return jnp.concatenate([prev_ref[q], z_ref[q, 0:CHUNK, :]], axis=0)
    return z_ref[q, c * CHUNK - 8:(c + 1) * CHUNK, :]


def _mixer_body(chunk_fn, refs, tb, nseq):
    prev_ref, st_ref = refs[-2:]

    @pl.when(pl.program_id(1) == 0)
    def _():
        prev_ref[...] = jnp.zeros_like(prev_ref)
        st_ref[...] = jnp.zeros_like(st_ref)

    carries = [{} for _ in range(nseq)]
    gens = [chunk_fn(q, c, carries[q], *refs, tb) for c in range(tb // CHUNK) for q in range(nseq)]
    _run_staggered(gens, lead=MIX_LEAD)


def _inproj_kernel(x_ref, lnw_ref, w_ref, za_ref, zb_ref, zc_ref, zba_ref):
    x = x_ref[...]
    h = x * lax.rsqrt(jnp.mean(x * x, axis=-1, keepdims=True) + RMS_EPS) * lnw_ref[...]
    hb = h.astype(BF16)
    o0 = RWKV_IN
    o1 = o0 + GDN_MAIN
    o2 = o1 + ATTN_IN
    za_ref[...] = jnp.dot(hb, w_ref[:, 0:o0], preferred_element_type=F32)
    zb_ref[...] = jnp.dot(hb, w_ref[:, o0:o1], preferred_element_type=F32)
    zc_ref[...] = jnp.dot(hb, w_ref[:, o1:o2], preferred_element_type=F32)
    zba_ref[...] = jnp.dot(hb, w_ref[:, o2:o2 + LANE], preferred_element_type=F32)


def _inproj(x2, lnw, w, tm):
    t = x2.shape[0]
    n = w.shape[1]
    row = lambda i: (i, 0)
    fixed = lambda i: (0, 0)
    return pl.pallas_call(
        _inproj_kernel,
        grid=(t // tm,),
        in_specs=[pl.BlockSpec((tm, D_MODEL), row),
                  pl.BlockSpec((1, D_MODEL), fixed),
                  pl.BlockSpec((D_MODEL, n), fixed, pipeline_mode=pl.Buffered(1))],
        out_specs=[pl.BlockSpec((tm, RWKV_IN), row),
                   pl.BlockSpec((tm, GDN_MAIN), row),
                   pl.BlockSpec((tm, ATTN_IN), row),
                   pl.BlockSpec((tm, LANE), row)],
        out_shape=[jax.ShapeDtypeStruct((t, RWKV_IN), F32),
                   jax.ShapeDtypeStruct((t, GDN_MAIN), F32),
                   jax.ShapeDtypeStruct((t, ATTN_IN), F32),
                   jax.ShapeDtypeStruct((t, LANE), F32)],
        compiler_params=pltpu.CompilerParams(dimension_semantics=("parallel",),
                                             vmem_limit_bytes=VMEM_LIMIT),
        name="inproj",
    )(x2, lnw, w)


def _rwkv_chunk(q, c, carry, z_ref, mu_ref, w0_ref, wup_ref, a0_ref, aup_ref, gup_ref, kk_ref, ka_ref, rk_ref,
                lnw_ref, lnb_ref, ho_ref, ho32_ref, tri_ref, cum_ref, o_ref, prev_ref, st_ref, tb):
    w = MIX_W
    nchunk = tb // CHUNK
    ext = _shifted_rows(z_ref, prev_ref, q, c)
    z = ext[8:]
    zprev = pltpu.roll(ext, 1, axis=0)[8:]
    zz = z + (zprev - z) * mu_ref[...]
    r = zz[:, 0:w]
    k = zz[:, w:2 * w]
    v = zz[:, 2 * w:3 * w]
    dwa = zz[:, 3 * w:3 * w + LANE]
    dg = zz[:, 3 * w + LANE:]
    yield
    log_w = -_softplus(-(w0_ref[...] + _dot(jnp.tanh(dwa), wup_ref[...]))) - 0.5
    ld = -jnp.exp(log_w)
    iclr = _sigmoid(a0_ref[...] + _dot(dwa, aup_ref[...]))
    gate = _dot(_sigmoid(dg), gup_ref[...])
    yield
    ho = ho_ref[...]
    kkv = k * kk_ref[...]
    k2 = k * (1.0 + (iclr - 1.0) * ka_ref[...])
    sums = _split_dot_r(jnp.concatenate([kkv * kkv, r * k2 * rk_ref[...]], axis=0), ho, 1)
    kkn = kkv * lax.rsqrt(sums[:CHUNK] + L2_EPS)
    bonus = sums[CHUNK:] * v
    b = kkn * iclr
    yield
    cum = _split_dot_l(cum_ref[...], ld, 2)
    e_neg = jnp.exp(-cum)
    atb = ((-kkn) * jnp.exp(cum - ld)).astype(BF16)
    bt = (b * e_neg).astype(BF16)
    kt = (k2 * e_neg).astype(BF16)
    rt = r * jnp.exp(cum)
    vb = v.astype(BF16)
    yield
    strict, incl = _tri_masks()
    g = lax.dot_general(jnp.concatenate([atb, rt.astype(BF16)], axis=0),
                        jnp.concatenate([_stack_heads(bt, ho), _stack_heads(kt, ho)], axis=0),
                        (((1,), (1,)), ((), ())), preferred_element_type=F32)
    a_ab = _stack_heads(jnp.where(strict, g[:CHUNK, :w], 0.0).astype(BF16), ho)
    a_ak = _stack_heads(jnp.where(strict, g[:CHUNK, w:], 0.0).astype(BF16), ho)
    a_rb = _stack_heads(jnp.where(incl, g[CHUNK:, :w], 0.0).astype(BF16), ho)
    a_rk = _stack_heads(jnp.where(incl, g[CHUNK:, w:], 0.0).astype(BF16), ho)
    yield
    t_inv = None
    for t_inv in _tri_inv_stages(a_ab, tri_ref):
        if t_inv is None:
            yield
    pv = jnp.dot(jnp.concatenate([a_ak, a_rk], axis=0), _stack_heads(vb, ho), preferred_element_type=F32)
    yield
    wu = jnp.dot(t_inv, jnp.concatenate([_stack_heads(atb, ho), pv[:w].astype(BF16)], axis=1),
                 preferred_element_type=F32)
    yield
    qy = jnp.dot(a_rb, wu.astype(BF16), preferred_element_type=F32)
    yield
    w2 = _unstack_heads(wu[:, :w])
    u0 = _unstack_heads(wu[:, w:])
    qp = rt + _unstack_heads(qy[:, :w])
    y0 = _unstack_heads(qy[:, w:] + pv[w:])
    last = cum[CHUNK - 1:CHUNK, :]
    to_end = jnp.exp(last - cum)
    upd_rhs = jnp.concatenate([b * to_end, k2 * to_end], axis=0)
    yield
    if c == 0:
        carry["ht"] = st_ref[q]
    y = _chain_step(carry, qp, y0, w2, u0, v, upd_rhs, jnp.exp(last), ho32_ref[...])
    if c == nchunk - 1:
        st_ref[q] = carry["ht"]
        prev_ref[q] = z_ref[q, tb - 8:tb, :]
    yield
    inv_n = 1.0 / HEAD_DIM
    mean = _split_dot_r(y, ho, 1) * inv_n
    d = y - mean
    yield
    var = _split_dot_r(d * d, ho, 1) * inv_n
    yn = d * lax.rsqrt(var + RWKV_GN_EPS) * lnw_ref[...] + lnb_ref[...]
    o_ref[q, c * CHUNK:(c + 1) * CHUNK, :] = ((yn + bonus) * gate).astype(o_ref.dtype)


def _rwkv_kernel(*refs, tb, nseq):
    _mixer_body(_rwkv_chunk, refs, tb, nseq)


def _rwkv(za, mu, w0, wup, a0, aup, gup, k_k, k_a, r_k, lnw, lnb, consts, tb, nseq):
    bsz, s, _ = za.shape
    vec = lambda n: pl.BlockSpec((1, n), lambda b, j: (0, 0))
    mat = lambda m, n: pl.BlockSpec((m, n), lambda b, j: (0, 0))
    return pl.pallas_call(
        functools.partial(_rwkv_kernel, tb=tb, nseq=nseq),
        grid=(bsz // nseq, s // tb),
        in_specs=[pl.BlockSpec((nseq, tb, RWKV_IN), lambda b, j: (b, j, 0)),
                  vec(RWKV_IN), vec(MIX_W), mat(LANE, MIX_W), vec(MIX_W), mat(LANE, MIX_W),
                  mat(RWKV_GATE_RANK, MIX_W), vec(MIX_W), vec(MIX_W), vec(MIX_W), vec(MIX_W), vec(MIX_W)]
        + _const_specs(),
        out_specs=pl.BlockSpec((nseq, tb, MIX_W), lambda b, j: (b, j, 0)),
        out_shape=jax.ShapeDtypeStruct((bsz, s, MIX_W), BF16),
        scratch_shapes=[pltpu.VMEM((nseq, 8, RWKV_IN), F32), pltpu.VMEM((nseq, MIX_W, MIX_W), F32)],
        compiler_params=pltpu.CompilerParams(dimension_semantics=("parallel", "arbitrary"),
                                             vmem_limit_bytes=VMEM_LIMIT),
        name="rwkv7",
    )(za, mu, w0, wup, a0, aup, gup, k_k, k_a, r_k, lnw, lnb, *consts)


def _gdn_chunk(q, c, carry, z_ref, ba_ref, cw_ref, alog_ref, dtb_ref, nw_ref, ho_ref, ho32_ref, tri_ref, cum_ref,
               o_ref, prev_ref, st_ref, tb):
    w = MIX_W
    nchunk = tb // CHUNK
    ext = _shifted_rows(z_ref, prev_ref, q, c)
    x = ext[:, :3 * w]
    gate = ext[8:, 3 * w:]
    acc = x[8:] * cw_ref[GDN_CONV - 1:GDN_CONV, :]
    for s in range(1, GDN_CONV):
        acc = acc + pltpu.roll(x, s, axis=0)[8:] * cw_ref[GDN_CONV - 1 - s:GDN_CONV - s, :]
    yield
    qkv = acc * _sigmoid(acc)
    qh = qkv[:, :w]
    k = qkv[:, w:2 * w]
    v = qkv[:, 2 * w:]
    ho = ho_ref[...]
    sums = _split_dot_r(jnp.concatenate([qh * qh, k * k], axis=0), ho, 1)
    qn = qh * lax.rsqrt(sums[:CHUNK] + L2_EPS) * (HEAD_DIM ** -0.5)
    kn = k * lax.rsqrt(sums[CHUNK:] + L2_EPS)
    yield
    ba = ba_ref[q, c * CHUNK:(c + 1) * CHUNK, :]
    er = _iota2((LANE, 2 * w), 0)
    ec = _iota2((LANE, 2 * w), 1)
    spread = (er == (ec % w) // HEAD_DIM + N_HEADS * (ec // w)).astype(BF16)
    ba_full = _split_dot_r(ba, spread, 2)
    beta = _sigmoid(ba_full[:, :w])
    g = -jnp.exp(alog_ref[...]) * _softplus(ba_full[:, w:] + dtb_ref[...])
    yield
    gc = _split_dot_l(cum_ref[...], g, 3)
    gt = gc.T
    eg = jnp.exp(gc)
    kb = kn * beta
    vbeta = (v * beta).astype(BF16)
    qe = qn * eg
    kbe = (kb * eg).astype(BF16)
    yield
    strict, incl = _tri_masks()
    s1 = lax.dot_general(jnp.concatenate([kb, qn], axis=0).astype(BF16), _stack_heads(kn.astype(BF16), ho),
                         (((1,), (1,)), ((), ())), preferred_element_type=F32)
    grow = jnp.concatenate([gt[h * HEAD_DIM:h * HEAD_DIM + 1, :] for h in range(N_HEADS)], axis=1)
    dec = jnp.exp(jnp.where(incl, gc - grow, -jnp.inf))
    lower = _stack_heads(jnp.where(strict, -(s1[:CHUNK] * dec), 0.0).astype(BF16), ho)
    intra = _stack_heads((s1[CHUNK:] * dec).astype(BF16), ho)
    yield
    t_inv = None
    for t_inv in _tri_inv_stages(lower, tri_ref):
        if t_inv is None:
            yield
    wu = jnp.dot(t_inv, jnp.concatenate([_stack_heads(kbe, ho), _stack_heads(vbeta, ho)], axis=1),
                 preferred_element_type=F32)
    yield
    qy = jnp.dot(intra, wu.astype(BF16), preferred_element_type=F32)
    yield
    w2 = -_unstack_heads(wu[:, :w])
    u0 = _unstack_heads(wu[:, w:])
    qp = qe - _unstack_heads(qy[:, :w])
    y0 = _unstack_heads(qy[:, w:])
    last = gc[CHUNK - 1:CHUNK, :]
    kdec = kn * jnp.exp(last - gc)
    yield
    if c == 0:
        carry["ht"] = st_ref[q]
    o = _chain_step(carry, qp, y0, w2, u0, None, kdec, jnp.exp(last), ho32_ref[...])
    if c == nchunk - 1:
        st_ref[q] = carry["ht"]
        prev_ref[q] = z_ref[q, tb - 8:tb, :]
    yield
    ms = _split_dot_r(o * o, ho, 1) * (1.0 / HEAD_DIM)
    on = o * lax.rsqrt(ms + RMS_EPS) * nw_ref[...]
    o_ref[q, c * CHUNK:(c + 1) * CHUNK, :] = (on * (gate * _sigmoid(gate))).astype(o_ref.dtype)


def _gdn_kernel(*refs, tb, nseq):
    _mixer_body(_gdn_chunk, refs, tb, nseq)


def _gdn(zb, zba, cw, alog, dtb, nw, consts, tb, nseq):
    bsz, s, _ = zb.shape
    vec = lambda n: pl.BlockSpec((1, n), lambda b, j: (0, 0))
    return pl.pallas_call(
        functools.partial(_gdn_kernel, tb=tb, nseq=nseq),
        grid=(bsz // nseq, s // tb),
        in_specs=[pl.BlockSpec((nseq, tb, GDN_MAIN), lambda b, j: (b, j, 0)),
                  pl.BlockSpec((nseq, tb, LANE), lambda b, j: (b, j, 0)),
                  pl.BlockSpec((GDN_CONV, 3 * MIX_W), lambda b, j: (0, 0)),
                  vec(MIX_W), vec(MIX_W), vec(MIX_W)] + _const_specs(),
        out_specs=pl.BlockSpec((nseq, tb, MIX_W), lambda b, j: (b, j, 0)),
        out_shape=jax.ShapeDtypeStruct((bsz, s, MIX_W), BF16),
        scratch_shapes=[pltpu.VMEM((nseq, 8, GDN_MAIN), F32), pltpu.VMEM((nseq, MIX_W, MIX_W), F32)],
        compiler_params=pltpu.CompilerParams(dimension_semantics=("parallel", "arbitrary"),
                                             vmem_limit_bytes=VMEM_LIMIT),
        name="gdn",
    )(zb, zba, cw, alog, dtb, nw, *consts)


def _t5_bucket_table():
    L = ATTN_BLOCK
    i = np.arange(L)[None, :]
    j = np.arange(2 * L)[:, None]
    dist = np.maximum(i + L - j, 0)
    max_exact = NUM_BUCKETS // 2
    nf = np.maximum(dist, max_exact).astype(np.float32)
    large = max_exact + (np.log(nf / max_exact) / math.log(MAX_DISTANCE / max_exact)
                         * (NUM_BUCKETS - max_exact)).astype(np.int32)
    large = np.minimum(large, NUM_BUCKETS - 1)
    return np.where(dist < max_exact, dist, large).astype(np.int32)


def _bias_kernel(bkt_ref, rel_ref, o_ref):
    L = ATTN_BLOCK
    grp = ATTN_Q_HEADS // ATTN_KV_HEADS
    bkt = bkt_ref[...]
    kj = _iota2(bkt.shape, 0)
    dist = _iota2(bkt.shape, 1) + L - kj
    in_window = (dist >= 0) & (dist < WINDOW)
    for h in range(ATTN_Q_HEADS):
        acc = jnp.zeros(bkt.shape, F32)
        for b in range(NUM_BUCKETS):
            acc = jnp.where(bkt == b, rel_ref[b, h], acc)
        cols = slice((h % grp) * L, (h % grp + 1) * L)
        o_ref[1, h // grp, :, cols] = jnp.where(in_window, acc, -jnp.inf)
        o_ref[0, h // grp, :, cols] = jnp.where(in_window & (kj >= L), acc, -jnp.inf)


def _bias_band(rel_bias):
    L = ATTN_BLOCK
    shape = (2, ATTN_KV_HEADS, 2 * L, (ATTN_Q_HEADS // ATTN_KV_HEADS) * L)
    return pl.pallas_call(
        _bias_kernel,
        in_specs=[pl.BlockSpec((2 * L, L), lambda: (0, 0)),
                  pl.BlockSpec(memory_space=pltpu.SMEM)],
        out_specs=pl.BlockSpec(shape, lambda: (0, 0, 0, 0)),
        out_shape=jax.ShapeDtypeStruct(shape, F32),
        name="t5_bias",
    )(jnp.asarray(_t5_bucket_table()), rel_bias)


def _swa_kernel(z_ref, bias0_ref, bias_ref, qnw_ref, knw_ref, sink_ref, o_ref, kprev_ref, vprev_ref, *, nsub):
    @pl.when(pl.program_id(1) == 0)
    def _():
        kprev_ref[...] = jnp.zeros_like(kprev_ref)
        vprev_ref[...] = jnp.zeros_like(vprev_ref)

    L = ATTN_BLOCK
    tq = nsub * L
    grp = ATTN_Q_HEADS // ATTN_KV_HEADS
    ngroups = ATTN_W // LANE
    z = z_ref[0]
    k = z[:, ATTN_W:ATTN_W + ATTN_KV_W]
    v = z[:, ATTN_W + ATTN_KV_W:]
    qg = [z[:, p * LANE:(p + 1) * LANE] for p in range(ngroups)]
    sq = _split_dot_r(jnp.concatenate([x * x for x in qg] + [k * k], axis=0), _head_ones(LANE), 2)
    inv_d = 1.0 / HEAD_DIM
    qscale = qnw_ref[...] * (HEAD_DIM ** -0.5)
    qn = [qg[p] * lax.rsqrt(sq[p * tq:(p + 1) * tq] * inv_d + RMS_EPS) * qscale for p in range(ngroups)]
    kn = k * lax.rsqrt(sq[ngroups * tq:] * inv_d + RMS_EPS) * knw_ref[...]
    kall = jnp.concatenate([kprev_ref[...], kn], axis=0)
    vall = jnp.concatenate([vprev_ref[...], v], axis=0)
    kprev_ref[...] = kn[tq - L:]
    vprev_ref[...] = v[tq - L:]
    kroll = pltpu.roll(kall, HEAD_DIM, axis=1)
    vroll = pltpu.roll(vall, HEAD_DIM, axis=1)
    lo_k = _iota2(kall.shape, 1) < HEAD_DIM
    lo_q = _iota2((L, LANE), 1) < HEAD_DIM
    kd = [jnp.where(lo_k, kall, kroll), jnp.where(lo_k, kroll, kall)]
    vdt = [jnp.where(lo_k, vall, vroll).T, jnp.where(lo_k, vroll, vall).T]

    qlane_head = _iota2((1, grp * L), 1) // L
    for i in range(nsub):
        rows = slice(i * L, (i + 1) * L)
        keys = slice(i * L, (i + 2) * L)
        outs = []
        for j in range(ATTN_KV_HEADS):
            qs = []
            for g in range(grp):
                hq = grp * j + g
                qrow = qn[hq // 2][rows]
                qs.append(jnp.where(lo_q, qrow, 0.0) if hq % 2 == 0 else jnp.where(lo_q, 0.0, qrow))
            bias = bias0_ref[0, j] if i == 0 else bias_ref[0, j]
            sc = _dot_nt(kd[j][keys], jnp.concatenate(qs, axis=0)) + bias
            sink = jnp.zeros((1, grp * L), F32)
            for g in range(grp):
                sink = jnp.where(qlane_head == g, sink_ref[grp * j + g], sink)
            m = jnp.maximum(jnp.max(sc, axis=0, keepdims=True), sink)
            pe = jnp.exp(sc - m)
            den = jnp.sum(pe, axis=0, keepdims=True) + jnp.exp(sink - m)
            o = (_dot(vdt[j][:, keys], pe) / den).T
            for g in range(0, grp, 2):
                outs.append(jnp.where(lo_q, o[g * L:(g + 1) * L], o[(g + 1) * L:(g + 2) * L]))
        o_ref[0, rows, :] = jnp.concatenate(outs, axis=1).astype(o_ref.dtype)


def _swa(zc, bias, qnw, knw, sinks, tq):
    bsz, s, _ = zc.shape
    L = ATTN_BLOCK
    bias_block = (1,) + bias.shape[1:]
    return pl.pallas_call(
        functools.partial(_swa_kernel, nsub=tq // L),
        grid=(bsz, s // tq),
        in_specs=[pl.BlockSpec((1, tq, ATTN_IN), lambda b, n: (b, n, 0)),
                  pl.BlockSpec(bias_block, lambda b, n: (jnp.minimum(n, 1), 0, 0, 0)),
                  pl.BlockSpec(bias_block, lambda b, n: (1, 0, 0, 0)),
                  pl.BlockSpec((1, LANE), lambda b, n: (0, 0)),
                  pl.BlockSpec((1, LANE), lambda b, n: (0, 0)),
                  pl.BlockSpec(memory_space=pltpu.SMEM)],
        out_specs=pl.BlockSpec((1, tq, ATTN_W), lambda b, n: (b, n, 0)),
        out_shape=jax.ShapeDtypeStruct((bsz, s, ATTN_W), BF16),
        scratch_shapes=[pltpu.VMEM((L, ATTN_KV_W), F32), pltpu.VMEM((L, ATTN_KV_W), F32)],
        compiler_params=pltpu.CompilerParams(dimension_semantics=("parallel", "arbitrary"),
                                             vmem_limit_bytes=VMEM_LIMIT),
        name="swa",
    )(zc, bias, bias, qnw, knw, sinks)


def _mlp_kernel(x_ref, ya_ref, yb_ref, yc_ref, wo_ref, ln2_ref, w1_ref, w2_ref, o_ref, *, tf):
    y = (jnp.dot(ya_ref[...], wo_ref[0:MIX_W, :], preferred_element_type=F32)
         + jnp.dot(yb_ref[...], wo_ref[MIX_W:2 * MIX_W, :], preferred_element_type=F32)
         + jnp.dot(yc_ref[...], wo_ref[2 * MIX_W:, :], preferred_element_type=F32))
    x1 = x_ref[...] + y
    h = (x1 * lax.rsqrt(jnp.mean(x1 * x1, axis=-1, keepdims=True) + RMS_EPS) * ln2_ref[...]).astype(BF16)
    acc = x1
    for j in range(D_FF // tf):
        u = jnp.dot(h, w1_ref[:, j * tf:(j + 1) * tf], preferred_element_type=F32)
        u = jnp.square(jnp.maximum(u, 0.0))
        acc = acc + jnp.dot(u.astype(BF16), w2_ref[j * tf:(j + 1) * tf, :], preferred_element_type=F32)
    o_ref[...] = acc


def _out_mlp(x2, ya, yb, yc, wo, ln2, w1, w2, tm, tf):
    t = x2.shape[0]
    row = lambda i: (i, 0)
    fixed = lambda shape: pl.BlockSpec(shape, lambda i: (0, 0), pipeline_mode=pl.Buffered(1))
    return pl.pallas_call(
        functools.partial(_mlp_kernel, tf=tf),
        grid=(t // tm,),
        in_specs=[pl.BlockSpec((tm, D_MODEL), row),
                  pl.BlockSpec((tm, MIX_W), row),
                  pl.BlockSpec((tm, MIX_W), row),
                  pl.BlockSpec((tm, ATTN_W), row),
                  fixed((D_MODEL, D_MODEL)),
                  fixed((1, D_MODEL)),
                  fixed((D_MODEL, D_FF)),
                  fixed((D_FF, D_MODEL))],
        out_specs=pl.BlockSpec((tm, D_MODEL), row),
        out_shape=jax.ShapeDtypeStruct((t, D_MODEL), F32),
        compiler_params=pltpu.CompilerParams(dimension_semantics=("parallel",),
                                             vmem_limit_bytes=VMEM_LIMIT),
        name="out_mlp",
    )(x2, ya, yb, yc, wo, ln2, w1, w2)


def _pad_rows(m, rows, offset):
    return jnp.zeros((rows, m.shape[1]), m.dtype).at[offset:offset + m.shape[0]].set(m)


def _per_head(p):
    return jnp.repeat(p, HEAD_DIM)[None, :]


def _layer(x2, bsz, s, bias, consts, p):
    t = bsz * s
    tb = min(MIX_TB, s)
    nseq = MIX_NSEQ if bsz % MIX_NSEQ == 0 else 1
    w_in = p["w_in"]
    o_b = RWKV_IN
    o_ba = o_b + GDN_MAIN
    o_c = o_b + GDN_IN
    w_re = jnp.concatenate(
        [w_in[:, :o_ba], w_in[:, o_c:], w_in[:, o_ba:o_c],
         jnp.zeros((D_MODEL, LANE - 2 * N_HEADS), w_in.dtype)], axis=1).astype(BF16)
    za, zb, zc, zba = _inproj(x2, p["ln1_w"][None, :], w_re, min(512, t))

    row = lambda a: a[None, :]
    ya = _rwkv(za.reshape(bsz, s, RWKV_IN), row(p["rwkv_mu"]), row(p["rwkv_w0"]),
               _pad_rows(p["rwkv_w_up"], LANE, 0).astype(BF16), row(p["rwkv_a0"]),
               _pad_rows(p["rwkv_a_up"], LANE, RWKV_DECAY_RANK).astype(BF16),
               p["rwkv_g_up"].astype(BF16), row(p["rwkv_k_k"]), row(p["rwkv_k_a"]),
               p["rwkv_r_k"].reshape(1, MIX_W), row(p["rwkv_lnx_w"]), row(p["rwkv_lnx_b"]),
               consts, tb, nseq)
    yb = _gdn(zb.reshape(bsz, s, GDN_MAIN), zba.reshape(bsz, s, LANE), p["gdn_conv_w"],
              _per_head(p["gdn_a_log"]), _per_head(p["gdn_dt_bias"]),
              jnp.tile(p["gdn_norm_w"], N_HEADS)[None, :], consts, tb, nseq)
    yc = _swa(zc.reshape(bsz, s, ATTN_IN), bias, jnp.tile(p["attn_q_norm_w"], 2)[None, :],
              jnp.tile(p["attn_k_norm_w"], 2)[None, :], p["attn_sinks"], min(SWA_TQ, s))
    return _out_mlp(x2, ya.reshape(t, MIX_W), yb.reshape(t, MIX_W), yc.reshape(t, ATTN_W),
                    p["w_out"].astype(BF16), p["ln2_w"][None, :], p["w_ff1"].astype(BF16),
                    p["w_ff2"].astype(BF16), min(512, t), min(1024, D_FF))


_LAYER_PARAMS = ("ln1_w", "w_in", "rwkv_mu", "rwkv_w0", "rwkv_w_up", "rwkv_a0", "rwkv_a_up", "rwkv_g_up",
                 "rwkv_k_k", "rwkv_k_a", "rwkv_r_k", "rwkv_lnx_w", "rwkv_lnx_b", "gdn_conv_w", "gdn_a_log",
                 "gdn_dt_bias", "gdn_norm_w", "attn_q_norm_w", "attn_k_norm_w", "attn_sinks",
                 "w_out", "ln2_w", "w_ff1", "w_ff2")


def kernel(x, ln1_w, w_in, rwkv_mu, rwkv_w0, rwkv_w_up, rwkv_a0, rwkv_a_up, rwkv_g_up, rwkv_k_k, rwkv_k_a,
           rwkv_r_k, rwkv_lnx_w, rwkv_lnx_b, gdn_conv_w, gdn_a_log, gdn_dt_bias, gdn_norm_w, attn_q_norm_w,
           attn_k_norm_w, attn_sinks, rel_bias, w_out, ln2_w, w_ff1, w_ff2):
    stacked = dict(ln1_w=ln1_w, w_in=w_in, rwkv_mu=rwkv_mu, rwkv_w0=rwkv_w0, rwkv_w_up=rwkv_w_up,
                   rwkv_a0=rwkv_a0, rwkv_a_up=rwkv_a_up, rwkv_g_up=rwkv_g_up, rwkv_k_k=rwkv_k_k,
                   rwkv_k_a=rwkv_k_a, rwkv_r_k=rwkv_r_k, rwkv_lnx_w=rwkv_lnx_w, rwkv_lnx_b=rwkv_lnx_b,
                   gdn_conv_w=gdn_conv_w, gdn_a_log=gdn_a_log, gdn_dt_bias=gdn_dt_bias, gdn_norm_w=gdn_norm_w,
                   attn_q_norm_w=attn_q_norm_w, attn_k_norm_w=attn_k_norm_w, attn_sinks=attn_sinks,
                   w_out=w_out, ln2_w=ln2_w, w_ff1=w_ff1, w_ff2=w_ff2)
    bsz, s, _ = x.shape
    bias = _bias_band(rel_bias)
    consts = _mixer_consts()
    x2 = x.reshape(bsz * s, D_MODEL)
    for l in range(w_in.shape[0]):
        x2 = _layer(x2, bsz, s, bias, consts, {name: stacked[name][l] for name in _LAYER_PARAMS})
    return x2.reshape(bsz, s, D_MODEL)
```

```python
import collections
import functools
import math

import numpy as np
import jax
import jax.numpy as jnp
from jax import lax
from jax.experimental import pallas as pl
from jax.experimental.pallas import tpu as pltpu

F32 = jnp.float32
BF16 = jnp.bfloat16

D_MODEL = 1024
DEPTH = 4
HEAD_DIM = 64
RMS_EPS = 1e-6
L2_EPS = 1e-6
N_HEADS = 4
MIX_W = N_HEADS * HEAD_DIM
RWKV_DECAY_RANK = 64
RWKV_ICLR_RANK = 64
RWKV_GATE_RANK = 128
RWKV_GN_EPS = 64e-5
RWKV_IN = 3 * MIX_W + RWKV_DECAY_RANK + RWKV_ICLR_RANK + RWKV_GATE_RANK
GDN_CONV = 4
GDN_MAIN = 4 * MIX_W
GDN_IN = GDN_MAIN + 2 * N_HEADS
ATTN_Q_HEADS = 8
ATTN_KV_HEADS = 2
ATTN_W = ATTN_Q_HEADS * HEAD_DIM
ATTN_KV_W = ATTN_KV_HEADS * HEAD_DIM
ATTN_IN = ATTN_W + 2 * ATTN_KV_W
ATTN_BLOCK = 128
WINDOW = 128
NUM_BUCKETS = 32
MAX_DISTANCE = 128
D_FF = 4 * D_MODEL
LANE = 128
BF16_ROWS = 16
CHUNK = 64
MIX_TB = 256
MIX_NSEQ = 2
MIX_CHUNK_LEAD = 2
MIX_SEQ_LEAD = 1
SWA_TQ = 512
VMEM_LIMIT = 56 * 1024 * 1024


def _dot(a, b):
    return jnp.dot(a.astype(BF16), b.astype(BF16), preferred_element_type=F32)


def _dot_nt(a, b):
    return lax.dot_general(a.astype(BF16), b.astype(BF16), (((1,), (1,)), ((), ())),
                           preferred_element_type=F32)


def _bf16_parts(x, parts):
    out = []
    r = x
    for p in range(parts):
        hi = r.astype(BF16)
        out.append(hi)
        if p + 1 < parts:
            r = r - hi.astype(F32)
    return out


def _split_dot_r(x, m_bf16, parts):
    rows = x.shape[0]
    t = jnp.dot(jnp.concatenate(_bf16_parts(x, parts), axis=0), m_bf16, preferred_element_type=F32)
    acc = t[:rows]
    for p in range(1, parts):
        acc = acc + t[p * rows:(p + 1) * rows]
    return acc


def _split_dot_l(m_bf16, x, parts):
    cols = x.shape[1]
    t = jnp.dot(m_bf16, jnp.concatenate(_bf16_parts(x, parts), axis=1), preferred_element_type=F32)
    acc = t[:, :cols]
    for p in range(1, parts):
        acc = acc + t[:, p * cols:(p + 1) * cols]
    return acc


def _sigmoid(x):
    return 1.0 / (1.0 + jnp.exp(-x))


def _softplus(x):
    return jnp.maximum(x, 0.0) + jnp.log(1.0 + jnp.exp(-jnp.abs(x)))


def _iota2(shape, dim):
    return lax.broadcasted_iota(jnp.int32, shape, dim)


def _head_ones(n):
    r = _iota2((n, n), 0) // HEAD_DIM
    c = _iota2((n, n), 1) // HEAD_DIM
    return (r == c).astype(BF16)


def _tri_masks():
    t = _iota2((CHUNK, MIX_W), 0)
    s = _iota2((CHUNK, MIX_W), 1) % HEAD_DIM
    return s < t, s <= t


def _run_staggered(gens, starts):
    done = [False] * len(gens)
    rnd = 0
    while not all(done):
        for i, g in enumerate(gens):
            if rnd >= starts[i] and not done[i]:
                try:
                    next(g)
                except StopIteration:
                    done[i] = True
        rnd += 1


MixConsts = collections.namedtuple("MixConsts", "ho ho32 tri cum")


def _mixer_consts():
    n = MIX_W
    r = np.arange(n)[:, None]
    c = np.arange(n)[None, :]
    ho = (r // HEAD_DIM) == (c // HEAD_DIM)
    tri = [r == c, (r // 2) == (c // 2)]
    s = 2
    while s < CHUNK:
        tri.append(((r // (2 * s)) == (c // (2 * s))) & ((r // s) % 2 == 1) & ((c // s) % 2 == 0))
        s *= 2
    cum = np.arange(CHUNK)[None, :] <= np.arange(CHUNK)[:, None]
    return MixConsts(jnp.asarray(ho, BF16), jnp.asarray(ho, F32), jnp.asarray(np.stack(tri), BF16),
                     jnp.asarray(cum, BF16))


def _const_specs():
    fixed = lambda shape: pl.BlockSpec(shape, lambda b, j: (0,) * len(shape), pipeline_mode=pl.Buffered(1))
    ntri = int(math.log2(CHUNK)) + 1
    return [fixed((MIX_W, MIX_W)), fixed((MIX_W, MIX_W)), fixed((ntri, MIX_W, MIX_W)), fixed((CHUNK, CHUNK))]


def _stack_heads(xb, ho):
    return jnp.concatenate([xb] * N_HEADS, axis=0) * ho


def _heads_to_rows(x, hi_half):
    lo = _iota2((x.shape[0], LANE), 1) < HEAD_DIM
    blocks = []
    for h in range(N_HEADS):
        col = x[:, (h // 2) * LANE:(h // 2 + 1) * LANE]
        if (h % 2 == 1) != hi_half:
            col = pltpu.roll(col, HEAD_DIM, axis=1)
        blocks.append(jnp.where(lo != hi_half, col, 0.0))
    return jnp.concatenate(blocks, axis=0)


def _rows_to_heads(z, hi_half):
    c = z.shape[0] // N_HEADS
    lo = _iota2((c, LANE), 1) < HEAD_DIM
    cols = []
    for p in range(N_HEADS // 2):
        even = z[2 * p * c:(2 * p + 1) * c]
        odd = z[(2 * p + 1) * c:(2 * p + 2) * c]
        if hi_half:
            even = pltpu.roll(even, HEAD_DIM, axis=1)
        else:
            odd = pltpu.roll(odd, HEAD_DIM, axis=1)
        cols.append(jnp.where(lo, even, odd))
    return jnp.concatenate(cols, axis=1)


def _tri_inv_stages(a, tri_ref):
    n = a.shape[0]
    t = tri_ref[0] + a * tri_ref[1]
    s = 2
    for lvl in range(2, tri_ref.shape[0]):
        a_off = a * tri_ref[lvl]
        if s % BF16_ROWS:
            inner = jnp.dot(a_off, t, preferred_element_type=F32).astype(BF16)
            yield None
            t = t + jnp.dot(t, inner, preferred_element_type=F32).astype(BF16)
        else:
            odd = [slice(r0 + s, r0 + 2 * s) for r0 in range(0, n, 2 * s)]
            zero = jnp.zeros((s, n), BF16)
            pick = lambda x: jnp.concatenate([x[r] for r in odd], axis=0)
            spread = lambda x: jnp.concatenate(
                [blk for i in range(len(odd)) for blk in (zero, x[i * s:(i + 1) * s])], axis=0)
            inner = spread(jnp.dot(pick(a_off), t, preferred_element_type=F32).astype(BF16))
            yield None
            t = t + spread(jnp.dot(pick(t), inner, preferred_element_type=F32).astype(BF16))
        yield None
        s *= 2
    yield t


def _chain_step(carry, qp, y0, w2, u0, upd_extra, upd_lhs_t, decay, ho32):
    h = carry["h"]
    uy = _dot(jnp.concatenate([w2, qp], axis=0), h)
    u = u0 + uy[:CHUNK]
    rows = u if upd_extra is None else jnp.concatenate([u, upd_extra], axis=0)
    carry["h"] = h * decay + jnp.dot(upd_lhs_t, rows.astype(BF16), preferred_element_type=F32) * ho32
    return uy[CHUNK:] + y0


def _shifted_rows(z_ref, prev_ref, q, c):
    if c == 0:
        return jnp.concatenate([prev_ref[q], z_ref[q, 0:CHUNK, :]], axis=0)
    return z_ref[q, c * CHUNK - 8:(c + 1) * CHUNK, :]


def _mixer_body(chunk_fn, refs, tb, nseq):
    prev_ref, st_ref = refs[-2:]

    @pl.when(pl.program_id(1) == 0)
    def _():
        prev_ref[...] = jnp.zeros_like(prev_ref)
        st_ref[...] = jnp.zeros_like(st_ref)

    carries = [{} for _ in range(nseq)]
    order = [(c, q) for c in range(tb // CHUNK) for q in range(nseq)]
    gens = [chunk_fn(q, c, carries[q], *refs, tb) for c, q in order]
    _run_staggered(gens, [c * MIX_CHUNK_LEAD + q * MIX_SEQ_LEAD for c, q in order])


def _inproj_kernel(x_ref, lnw_ref, w_ref, za_ref, zb_ref, zc_ref, zba_ref):
    x = x_ref[...]
    h = x * lax.rsqrt(jnp.mean(x * x, axis=-1, keepdims=True) + RMS_EPS) * lnw_ref[...]
    hb = h.astype(BF16)
    o0 = RWKV_IN
    o1 = o0 + GDN_MAIN
    o2 = o1 + ATTN_IN
    za_ref[...] = jnp.dot(hb, w_ref[:, 0:o0], preferred_element_type=F32)
    zb_ref[...] = jnp.dot(hb, w_ref[:, o0:o1], preferred_element_type=F32)
    zc_ref[...] = jnp.dot(hb, w_ref[:, o1:o2], preferred_element_type=F32)
    zba_ref[...] = jnp.dot(hb, w_ref[:, o2:o2 + LANE], preferred_element_type=F32)


def _inproj(x2, lnw, w, tm):
    t = x2.shape[0]
    n = w.shape[1]
    row = lambda i: (i, 0)
    fixed = lambda i: (0, 0)
    return pl.pallas_call(
        _inproj_kernel,
        grid=(t // tm,),
        in_specs=[pl.BlockSpec((tm, D_MODEL), row),
                  pl.BlockSpec((1, D_MODEL), fixed),
                  pl.BlockSpec((D_MODEL, n), fixed, pipeline_mode=pl.Buffered(1))],
        out_specs=[pl.BlockSpec((tm, RWKV_IN), row),
                   pl.BlockSpec((tm, GDN_MAIN), row),
                   pl.BlockSpec((tm, ATTN_IN), row),
                   pl.BlockSpec((tm, LANE), row)],
        out_shape=[jax.ShapeDtypeStruct((t, RWKV_IN), F32),
                   jax.ShapeDtypeStruct((t, GDN_MAIN), F32),
                   jax.ShapeDtypeStruct((t, ATTN_IN), F32),
                   jax.ShapeDtypeStruct((t, LANE), F32)],
        compiler_params=pltpu.CompilerParams(dimension_semantics=("parallel",),
                                             vmem_limit_bytes=VMEM_LIMIT),
        name="inproj",
    )(x2, lnw, w)


def _rwkv_chunk(q, c, carry, z_ref, mu_ref, w0_ref, wup_ref, a0_ref, aup_ref, gup_ref, kk_ref, ka_ref, rk_ref,
                lnw_ref, lnb_ref, ho_ref, ho32_ref, tri_ref, cum_ref, o_ref, prev_ref, st_ref, tb):
    w = MIX_W
    nchunk = tb // CHUNK
    ext = _shifted_rows(z_ref, prev_ref, q, c)
    z = ext[8:]
    zprev = pltpu.roll(ext, 1, axis=0)[8:]
    zz = z + (zprev - z) * mu_ref[...]
    r = zz[:, 0:w]
    k = zz[:, w:2 * w]
    v = zz[:, 2 * w:3 * w]
    dwa = zz[:, 3 * w:3 * w + LANE]
    dg = zz[:, 3 * w + LANE:]
    yield
    log_w = -_softplus(-(w0_ref[...] + _dot(jnp.tanh(dwa), wup_ref[...]))) - 0.5
    ld = -jnp.exp(log_w)
    iclr = _sigmoid(a0_ref[...] + _dot(dwa, aup_ref[...]))
    gate = _dot(_sigmoid(dg), gup_ref[...])
    yield
    ho = ho_ref[...]
    kkv = k * kk_ref[...]
    k2 = k * (1.0 + (iclr - 1.0) * ka_ref[...])
    sums = _split_dot_r(jnp.concatenate([kkv * kkv, r * k2 * rk_ref[...]], axis=0), ho, 1)
    kkn = kkv * lax.rsqrt(sums[:CHUNK] + L2_EPS)
    bonus = sums[CHUNK:] * v
    b = kkn * iclr
    yield
    cum = _split_dot_l(cum_ref[...], ld, 2)
    e_neg = jnp.exp(-cum)
    at = (-kkn) * jnp.exp(cum - ld)
    atb = at.astype(BF16)
    bt = (b * e_neg).astype(BF16)
    kt = (k2 * e_neg).astype(BF16)
    rt = r * jnp.exp(cum)
    at_rows = _heads_to_rows(at, False)
    v_rows = _heads_to_rows(v, True).astype(BF16)
    yield
    strict, incl = _tri_masks()
    g = lax.dot_general(jnp.concatenate([atb, rt.astype(BF16)], axis=0),
                        jnp.concatenate([_stack_heads(bt, ho), _stack_heads(kt, ho)], axis=0),
                        (((1,), (1,)), ((), ())), preferred_element_type=F32)
    a_ab = _stack_heads(jnp.where(strict, g[:CHUNK, :w], 0.0).astype(BF16), ho)
    a_ak = _stack_heads(jnp.where(strict, g[:CHUNK, w:], 0.0).astype(BF16), ho)
    a_rb = _stack_heads(jnp.where(incl, g[CHUNK:, :w], 0.0).astype(BF16), ho)
    a_rk = _stack_heads(jnp.where(incl, g[CHUNK:, w:], 0.0).astype(BF16), ho)
    yield
    t_inv = None
    for t_inv in _tri_inv_stages(a_ab, tri_ref):
        if t_inv is None:
            yield
    pv = jnp.dot(jnp.concatenate([a_ak, a_rk], axis=0), v_rows, preferred_element_type=F32)
    yield
    wu = jnp.dot(t_inv, (at_rows + pv[:w]).astype(BF16), preferred_element_type=F32)
    yield
    qy = jnp.dot(a_rb, wu.astype(BF16), preferred_element_type=F32) + pv[w:]
    yield
    w2 = _rows_to_heads(wu, False)
    u0 = _rows_to_heads(wu, True)
    qp = rt + _rows_to_heads(qy, False)
    y0 = _rows_to_heads(qy, True)
    last = cum[CHUNK - 1:CHUNK, :]
    to_end = jnp.exp(last - cum)
    upd_lhs_t = jnp.concatenate([b * to_end, k2 * to_end], axis=0).T.astype(BF16)
    decay_col = jnp.broadcast_to(jnp.exp(last), (8, w)).T[:, 0:1]
    yield
    if c == 0:
        carry["h"] = st_ref[q]
    y = _chain_step(carry, qp, y0, w2, u0, v, upd_lhs_t, decay_col, ho32_ref[...])
    if c == nchunk - 1:
        st_ref[q] = carry["h"]
        prev_ref[q] = z_ref[q, tb - 8:tb, :]
    yield
    inv_n = 1.0 / HEAD_DIM
    mean = _split_dot_r(y, ho, 1) * inv_n
    d = y - mean
    yield
    var = _split_dot_r(d * d, ho, 1) * inv_n
    yn = d * lax.rsqrt(var + RWKV_GN_EPS) * lnw_ref[...] + lnb_ref[...]
    o_ref[q, c * CHUNK:(c + 1) * CHUNK, :] = ((yn + bonus) * gate).astype(o_ref.dtype)


def _rwkv_kernel(*refs, tb, nseq):
    _mixer_body(_rwkv_chunk, refs, tb, nseq)


def _rwkv(za, mu, w0, wup, a0, aup, gup, k_k, k_a, r_k, lnw, lnb, consts, tb, nseq):
    bsz, s, _ = za.shape
    vec = lambda n: pl.BlockSpec((1, n), lambda b, j: (0, 0))
    mat = lambda m, n: pl.BlockSpec((m, n), lambda b, j: (0, 0))
    return pl.pallas_call(
        functools.partial(_rwkv_kernel, tb=tb, nseq=nseq),
        grid=(bsz // nseq, s // tb),
        in_specs=[pl.BlockSpec((nseq, tb, RWKV_IN), lambda b, j: (b, j, 0)),
                  vec(RWKV_IN), vec(MIX_W), mat(LANE, MIX_W), vec(MIX_W), mat(LANE, MIX_W),
                  mat(RWKV_GATE_RANK, MIX_W), vec(MIX_W), vec(MIX_W), vec(MIX_W), vec(MIX_W), vec(MIX_W)]
        + _const_specs(),
        out_specs=pl.BlockSpec((nseq, tb, MIX_W), lambda b, j: (b, j, 0)),
        out_shape=jax.ShapeDtypeStruct((bsz, s, MIX_W), BF16),
        scratch_shapes=[pltpu.VMEM((nseq, 8, RWKV_IN), F32), pltpu.VMEM((nseq, MIX_W, MIX_W), F32)],
        compiler_params=pltpu.CompilerParams(dimension_semantics=("parallel", "arbitrary"),
                                             vmem_limit_bytes=VMEM_LIMIT),
        name="rwkv7",
    )(za, mu, w0, wup, a0, aup, gup, k_k, k_a, r_k, lnw, lnb, *consts)


def _gdn_chunk(q, c, carry, z_ref, ba_ref, cw_ref, alog_ref, dtb_ref, nw_ref, ho_ref, ho32_ref, tri_ref, cum_ref,
               o_ref, prev_ref, st_ref, tb):
    w = MIX_W
    nchunk = tb // CHUNK
    ext = _shifted_rows(z_ref, prev_ref, q, c)
    x = ext[:, :3 * w]
    gate = ext[8:, 3 * w:]
    acc = x[8:] * cw_ref[GDN_CONV - 1:GDN_CONV, :]
    for s in range(1, GDN_CONV):
        acc = acc + pltpu.roll(x, s, axis=0)[8:] * cw_ref[GDN_CONV - 1 - s:GDN_CONV - s, :]
    yield
    qkv = acc * _sigmoid(acc)
    qh = qkv[:, :w]
    k = qkv[:, w:2 * w]
    v = qkv[:, 2 * w:]
    ho = ho_ref[...]
    sums = _split_dot_r(jnp.concatenate([qh * qh, k * k], axis=0), ho, 1)
    qn = qh * lax.rsqrt(sums[:CHUNK] + L2_EPS) * (HEAD_DIM ** -0.5)
    kn = k * lax.rsqrt(sums[CHUNK:] + L2_EPS)
    yield
    ba = ba_ref[q, c * CHUNK:(c + 1) * CHUNK, :]
    er = _iota2((LANE, 2 * w), 0)
    ec = _iota2((LANE, 2 * w), 1)
    spread = (er == (ec % w) // HEAD_DIM + N_HEADS * (ec // w)).astype(BF16)
    ba_full = _split_dot_r(ba, spread, 2)
    beta = _sigmoid(ba_full[:, :w])
    g = -jnp.exp(alog_ref[...]) * _softplus(ba_full[:, w:] + dtb_ref[...])
    yield
    gc = _split_dot_l(cum_ref[...], g, 3)
    gt = gc.T
    eg = jnp.exp(gc)
    kb = kn * beta
    qe = qn * eg
    kv_rows = (_heads_to_rows(kb * eg, False) + _heads_to_rows(v * beta, True)).astype(BF16)
    yield
    strict, incl = _tri_masks()
    s1 = lax.dot_general(jnp.concatenate([kb, qn], axis=0).astype(BF16), _stack_heads(kn.astype(BF16), ho),
                         (((1,), (1,)), ((), ())), preferred_element_type=F32)
    grow = jnp.concatenate([gt[h * HEAD_DIM:h * HEAD_DIM + 1, :] for h in range(N_HEADS)], axis=1)
    dec = jnp.exp(jnp.where(incl, gc - grow, -jnp.inf))
    lower = _stack_heads(jnp.where(strict, -(s1[:CHUNK] * dec), 0.0).astype(BF16), ho)
    intra = _stack_heads((s1[CHUNK:] * dec).astype(BF16), ho)
    yield
    t_inv = None
    for t_inv in _tri_inv_stages(lower, tri_ref):
        if t_inv is None:
            yield
    wu = jnp.dot(t_inv, kv_rows, preferred_element_type=F32)
    yield
    qy = jnp.dot(intra, wu.astype(BF16), preferred_element_type=F32)
    yield
    w2 = -_rows_to_heads(wu, False)
    u0 = _rows_to_heads(wu, True)
    qp = qe - _rows_to_heads(qy, False)
    y0 = _rows_to_heads(qy, True)
    last = gc[CHUNK - 1:CHUNK, :]
    kdec_t = (kn * jnp.exp(last - gc)).T.astype(BF16)
    yield
    if c == 0:
        carry["h"] = st_ref[q]
    o = _chain_step(carry, qp, y0, w2, u0, None, kdec_t, jnp.exp(last), ho32_ref[...])
    if c == nchunk - 1:
        st_ref[q] = carry["h"]
        prev_ref[q] = z_ref[q, tb - 8:tb, :]
    yield
    ms = _split_dot_r(o * o, ho, 1) * (1.0 / HEAD_DIM)
    on = o * lax.rsqrt(ms + RMS_EPS) * nw_ref[...]
    o_ref[q, c * CHUNK:(c + 1) * CHUNK, :] = (on * (gate * _sigmoid(gate))).astype(o_ref.dtype)


def _gdn_kernel(*refs, tb, nseq):
    _mixer_body(_gdn_chunk, refs, tb, nseq)


def _gdn(zb, zba, cw, alog, dtb, nw, consts, tb, nseq):
    bsz, s, _ = zb.shape
    vec = lambda n: pl.BlockSpec((1, n), lambda b, j: (0, 0))
    return pl.pallas_call(
        functools.partial(_gdn_kernel, tb=tb, nseq=nseq),
        grid=(bsz // nseq, s // tb),
        in_specs=[pl.BlockSpec((nseq, tb, GDN_MAIN), lambda b, j: (b, j, 0)),
                  pl.BlockSpec((nseq, tb, LANE), lambda b, j: (b, j, 0)),
                  pl.BlockSpec((GDN_CONV, 3 * MIX_W), lambda b, j: (0, 0)),
                  vec(MIX_W), vec(MIX_W), vec(MIX_W)] + _const_specs(),
        out_specs=pl.BlockSpec((nseq, tb, MIX_W), lambda b, j: (b, j, 0)),
        out_shape=jax.ShapeDtypeStruct((bsz, s, MIX_W), BF16),
        scratch_shapes=[pltpu.VMEM((nseq, 8, GDN_MAIN), F32), pltpu.VMEM((nseq, MIX_W, MIX_W), F32)],
        compiler_params=pltpu.CompilerParams(dimension_semantics=("parallel", "arbitrary"),
                                             vmem_limit_bytes=VMEM_LIMIT),
        name="gdn",
    )(zb, zba, cw, alog, dtb, nw, *consts)


def _t5_bucket_table():
    L = ATTN_BLOCK
    i = np.arange(L)[None, :]
    j = np.arange(2 * L)[:, None]
    dist = np.maximum(i + L - j, 0)
    max_exact = NUM_BUCKETS // 2
    nf = np.maximum(dist, max_exact).astype(np.float32)
    large = max_exact + (np.log(nf / max_exact) / math.log(MAX_DISTANCE / max_exact)
                         * (NUM_BUCKETS - max_exact)).astype(np.int32)
    large = np.minimum(large, NUM_BUCKETS - 1)
    return np.where(dist < max_exact, dist, large).astype(np.int32)


def _bias_kernel(bkt_ref, rel_ref, o_ref):
    L = ATTN_BLOCK
    grp = ATTN_Q_HEADS // ATTN_KV_HEADS
    bkt = bkt_ref[...]
    kj = _iota2(bkt.shape, 0)
    dist = _iota2(bkt.shape, 1) + L - kj
    in_window = (dist >= 0) & (dist < WINDOW)
    for h in range(ATTN_Q_HEADS):
        acc = jnp.zeros(bkt.shape, F32)
        for b in range(NUM_BUCKETS):
            acc = jnp.where(bkt == b, rel_ref[b, h], acc)
        cols = slice((h % grp) * L, (h % grp + 1) * L)
        o_ref[1, h // grp, :, cols] = jnp.where(in_window, acc, -jnp.inf)
        o_ref[0, h // grp, :, cols] = jnp.where(in_window & (kj >= L), acc, -jnp.inf)


def _bias_band(rel_bias):
    L = ATTN_BLOCK
    shape = (2, ATTN_KV_HEADS, 2 * L, (ATTN_Q_HEADS // ATTN_KV_HEADS) * L)
    return pl.pallas_call(
        _bias_kernel,
        in_specs=[pl.BlockSpec((2 * L, L), lambda: (0, 0)),
                  pl.BlockSpec(memory_space=pltpu.SMEM)],
        out_specs=pl.BlockSpec(shape, lambda: (0, 0, 0, 0)),
        out_shape=jax.ShapeDtypeStruct(shape, F32),
        name="t5_bias",
    )(jnp.asarray(_t5_bucket_table()), rel_bias)


def _swa_kernel(z_ref, bias0_ref, bias_ref, qnw_ref, knw_ref, sink_ref, o_ref, kprev_ref, vprev_ref, *, nsub):
    @pl.when(pl.program_id(1) == 0)
    def _():
        kprev_ref[...] = jnp.zeros_like(kprev_ref)
        vprev_ref[...] = jnp.zeros_like(vprev_ref)

    L = ATTN_BLOCK
    tq = nsub * L
    grp = ATTN_Q_HEADS // ATTN_KV_HEADS
    ngroups = ATTN_W // LANE
    z = z_ref[0]
    k = z[:, ATTN_W:ATTN_W + ATTN_KV_W]
    v = z[:, ATTN_W + ATTN_KV_W:]
    qg = [z[:, p * LANE:(p + 1) * LANE] for p in range(ngroups)]
    sq = _split_dot_r(jnp.concatenate([x * x for x in qg] + [k * k], axis=0), _head_ones(LANE), 2)
    inv_d = 1.0 / HEAD_DIM
    qscale = qnw_ref[...] * (HEAD_DIM ** -0.5)
    qn = [qg[p] * lax.rsqrt(sq[p * tq:(p + 1) * tq] * inv_d + RMS_EPS) * qscale for p in range(ngroups)]
    kn = k * lax.rsqrt(sq[ngroups * tq:] * inv_d + RMS_EPS) * knw_ref[...]
    kall = jnp.concatenate([kprev_ref[...], kn], axis=0)
    vall = jnp.concatenate([vprev_ref[...], v], axis=0)
    kprev_ref[...] = kn[tq - L:]
    vprev_ref[...] = v[tq - L:]
    kroll = pltpu.roll(kall, HEAD_DIM, axis=1)
    vroll = pltpu.roll(vall, HEAD_DIM, axis=1)
    lo_k = _iota2(kall.shape, 1) < HEAD_DIM
    lo_q = _iota2((L, LANE), 1) < HEAD_DIM
    kd = [jnp.where(lo_k, kall, kroll), jnp.where(lo_k, kroll, kall)]
    vdt = [jnp.where(lo_k, vall, vroll).T, jnp.where(lo_k, vroll, vall).T]

    qlane_head = _iota2((1, grp * L), 1) // L

    def block_head(i, j):
        rows = slice(i * L, (i + 1) * L)
        keys = slice(i * L, (i + 2) * L)
        qs = []
        for g in range(grp):
            hq = grp * j + g
            qrow = qn[hq // 2][rows]
            qs.append(jnp.where(lo_q, qrow, 0.0) if hq % 2 == 0 else jnp.where(lo_q, 0.0, qrow))
        bias = bias0_ref[0, j] if i == 0 else bias_ref[0, j]
        sc = _dot_nt(kd[j][keys], jnp.concatenate(qs, axis=0)) + bias
        yield
        sink = jnp.zeros((1, grp * L), F32)
        for g in range(grp):
            sink = jnp.where(qlane_head == g, sink_ref[grp * j + g], sink)
        m = jnp.maximum(jnp.max(sc, axis=0, keepdims=True), sink)
        pe = jnp.exp(sc - m)
        den = jnp.sum(pe, axis=0, keepdims=True) + jnp.exp(sink - m)
        yield
        ot = _dot(vdt[j][:, keys], pe) / den
        yield
        o = ot.T
        for g in range(0, grp, 2):
            o_ref[0, rows, (grp * j + g) // 2 * LANE:((grp * j + g) // 2 + 1) * LANE] = jnp.where(
                lo_q, o[g * L:(g + 1) * L], o[(g + 1) * L:(g + 2) * L]).astype(o_ref.dtype)

    gens = [block_head(i, j) for i in range(nsub) for j in range(ATTN_KV_HEADS)]
    _run_staggered(gens, list(range(len(gens))))


def _swa(zc, bias, qnw, knw, sinks, tq):
    bsz, s, _ = zc.shape
    L = ATTN_BLOCK
    bias_block = (1,) + bias.shape[1:]
    return pl.pallas_call(
        functools.partial(_swa_kernel, nsub=tq // L),
        grid=(bsz, s // tq),
        in_specs=[pl.BlockSpec((1, tq, ATTN_IN), lambda b, n: (b, n, 0)),
                  pl.BlockSpec(bias_block, lambda b, n: (jnp.minimum(n, 1), 0, 0, 0)),
                  pl.BlockSpec(bias_block, lambda b, n: (1, 0, 0, 0)),
                  pl.BlockSpec((1, LANE), lambda b, n: (0, 0)),
                  pl.BlockSpec((1, LANE), lambda b, n: (0, 0)),
                  pl.BlockSpec(memory_space=pltpu.SMEM)],
        out_specs=pl.BlockSpec((1, tq, ATTN_W), lambda b, n: (b, n, 0)),
        out_shape=jax.ShapeDtypeStruct((bsz, s, ATTN_W), BF16),
        scratch_shapes=[pltpu.VMEM((L, ATTN_KV_W), F32), pltpu.VMEM((L, ATTN_KV_W), F32)],
        compiler_params=pltpu.CompilerParams(dimension_semantics=("parallel", "arbitrary"),
                                             vmem_limit_bytes=VMEM_LIMIT),
        name="swa",
    )(zc, bias, bias, qnw, knw, sinks)


def _mlp_kernel(x_ref, ya_ref, yb_ref, yc_ref, wo_ref, ln2_ref, w1_ref, w2_ref, o_ref, *, tf):
    y = (jnp.dot(ya_ref[...], wo_ref[0:MIX_W, :], preferred_element_type=F32)
         + jnp.dot(yb_ref[...], wo_ref[MIX_W:2 * MIX_W, :], preferred_element_type=F32)
         + jnp.dot(yc_ref[...], wo_ref[2 * MIX_W:, :], preferred_element_type=F32))
    x1 = x_ref[...] + y
    h = (x1 * lax.rsqrt(jnp.mean(x1 * x1, axis=-1, keepdims=True) + RMS_EPS) * ln2_ref[...]).astype(BF16)
    acc = x1
    for j in range(D_FF // tf):
        u = jnp.dot(h, w1_ref[:, j * tf:(j + 1) * tf], preferred_element_type=F32)
        u = jnp.square(jnp.maximum(u, 0.0))
        acc = acc + jnp.dot(u.astype(BF16), w2_ref[j * tf:(j + 1) * tf, :], preferred_element_type=F32)
    o_ref[...] = acc


def _out_mlp(x2, ya, yb, yc, wo, ln2, w1, w2, tm, tf):
    t = x2.shape[0]
    row = lambda i: (i, 0)
    fixed = lambda shape: pl.BlockSpec(shape, lambda i: (0, 0), pipeline_mode=pl.Buffered(1))
    return pl.pallas_call(
        functools.partial(_mlp_kernel, tf=tf),
        grid=(t // tm,),
        in_specs=[pl.BlockSpec((tm, D_MODEL), row),
                  pl.BlockSpec((tm, MIX_W), row),
                  pl.BlockSpec((tm, MIX_W), row),
                  pl.BlockSpec((tm, ATTN_W), row),
                  fixed((D_MODEL, D_MODEL)),
                  fixed((1, D_MODEL)),
                  fixed((D_MODEL, D_FF)),
                  fixed((D_FF, D_MODEL))],
        out_specs=pl.BlockSpec((tm, D_MODEL), row),
        out_shape=jax.ShapeDtypeStruct((t, D_MODEL), F32),
        compiler_params=pltpu.CompilerParams(dimension_semantics=("parallel",),
                                             vmem_limit_bytes=VMEM_LIMIT),
        name="out_mlp",
    )(x2, ya, yb, yc, wo, ln2, w1, w2)


def _pad_rows(m, rows, offset):
    return jnp.zeros((rows, m.shape[1]), m.dtype).at[offset:offset + m.shape[0]].set(m)


def _per_head(p):
    return jnp.repeat(p, HEAD_DIM)[None, :]


def _layer(x2, bsz, s, bias, consts, p):
    t = bsz * s
    tb = min(MIX_TB, s)
    nseq = MIX_NSEQ if bsz % MIX_NSEQ == 0 else 1
    w_in = p["w_in"]
    o_b = RWKV_IN
    o_ba = o_b + GDN_MAIN
    o_c = o_b + GDN_IN
    w_re = jnp.concatenate(
        [w_in[:, :o_ba], w_in[:, o_c:], w_in[:, o_ba:o_c],
         jnp.zeros((D_MODEL, LANE - 2 * N_HEADS), w_in.dtype)], axis=1).astype(BF16)
    za, zb, zc, zba = _inproj(x2, p["ln1_w"][None, :], w_re, min(512, t))

    row = lambda a: a[None, :]
    ya = _rwkv(za.reshape(bsz, s, RWKV_IN), row(p["rwkv_mu"]), row(p["rwkv_w0"]),
               _pad_rows(p["rwkv_w_up"], LANE, 0).astype(BF16), row(p["rwkv_a0"]),
               _pad_rows(p["rwkv_a_up"], LANE, RWKV_DECAY_RANK).astype(BF16),
               p["rwkv_g_up"].astype(BF16), row(p["rwkv_k_k"]), row(p["rwkv_k_a"]),
               p["rwkv_r_k"].reshape(1, MIX_W), row(p["rwkv_lnx_w"]), row(p["rwkv_lnx_b"]),
               consts, tb, nseq)
    yb = _gdn(zb.reshape(bsz, s, GDN_MAIN), zba.reshape(bsz, s, LANE), p["gdn_conv_w"],
              _per_head(p["gdn_a_log"]), _per_head(p["gdn_dt_bias"]),
              jnp.tile(p["gdn_norm_w"], N_HEADS)[None, :], consts, tb, nseq)
    yc = _swa(zc.reshape(bsz, s, ATTN_IN), bias, jnp.tile(p["attn_q_norm_w"], 2)[None, :],
              jnp.tile(p["attn_k_norm_w"], 2)[None, :], p["attn_sinks"], min(SWA_TQ, s))
    return _out_mlp(x2, ya.reshape(t, MIX_W), yb.reshape(t, MIX_W), yc.reshape(t, ATTN_W),
                    p["w_out"].astype(BF16), p["ln2_w"][None, :], p["w_ff1"].astype(BF16),
                    p["w_ff2"].astype(BF16), min(512, t), min(1024, D_FF))


_LAYER_PARAMS = ("ln1_w", "w_in", "rwkv_mu", "rwkv_w0", "rwkv_w_up", "rwkv_a0", "rwkv_a_up", "rwkv_g_up",
                 "rwkv_k_k", "rwkv_k_a", "rwkv_r_k", "rwkv_lnx_w", "rwkv_lnx_b", "gdn_conv_w", "gdn_a_log",
                 "gdn_dt_bias", "gdn_norm_w", "attn_q_norm_w", "attn_k_norm_w", "attn_sinks",
                 "w_out", "ln2_w", "w_ff1", "w_ff2")


def kernel(x, ln1_w, w_in, rwkv_mu, rwkv_w0, rwkv_w_up, rwkv_a0, rwkv_a_up, rwkv_g_up, rwkv_k_k, rwkv_k_a,
           rwkv_r_k, rwkv_lnx_w, rwkv_lnx_b, gdn_conv_w, gdn_a_log, gdn_dt_bias, gdn_norm_w, attn_q_norm_w,
           attn_k_norm_w, attn_sinks, rel_bias, w_out, ln2_w, w_ff1, w_ff2):
    stacked = dict(ln1_w=ln1_w, w_in=w_in, rwkv_mu=rwkv_mu, rwkv_w0=rwkv_w0, rwkv_w_up=rwkv_w_up,
                   rwkv_a0=rwkv_a0, rwkv_a_up=rwkv_a_up, rwkv_g_up=rwkv_g_up, rwkv_k_k=rwkv_k_k,
                   rwkv_k_a=rwkv_k_a, rwkv_r_k=rwkv_r_k, rwkv_lnx_w=rwkv_lnx_w, rwkv_lnx_b=rwkv_lnx_b,
                   gdn_conv_w=gdn_conv_w, gdn_a_log=gdn_a_log, gdn_dt_bias=gdn_dt_bias, gdn_norm_w=gdn_norm_w,
                   attn_q_norm_w=attn_q_norm_w, attn_k_norm_w=attn_k_norm_w, attn_sinks=attn_sinks,
                   w_out=w_out, ln2_w=ln2_w, w_ff1=w_ff1, w_ff2=w_ff2)
    bsz, s, _ = x.shape
    bias = _bias_band(rel_bias)
    consts = _mixer_consts()
    x2 = x.reshape(bsz * s, D_MODEL)
    for l in range(w_in.shape[0]):
        x2 = _layer(x2, bsz, s, bias, consts, {name: stacked[name][l] for name in _LAYER_PARAMS})
    return x2.reshape(bsz, s, D_MODEL)
```

```python
import collections
import functools
import math

import numpy as np
import jax
import jax.numpy as jnp
from jax import lax
from jax.experimental import pallas as pl
from jax.experimental.pallas import tpu as pltpu

F32 = jnp.float32
BF16 = jnp.bfloat16

D_MODEL = 1024
DEPTH = 4
HEAD_DIM = 64
RMS_EPS = 1e-6
L2_EPS = 1e-6
N_HEADS = 4
MIX_W = N_HEADS * HEAD_DIM
RWKV_DECAY_RANK = 64
RWKV_ICLR_RANK = 64
RWKV_GATE_RANK = 128
RWKV_GN_EPS = 64e-5
RWKV_IN = 3 * MIX_W + RWKV_DECAY_RANK + RWKV_ICLR_RANK + RWKV_GATE_RANK
GDN_CONV = 4
GDN_MAIN = 4 * MIX_W
GDN_IN = GDN_MAIN + 2 * N_HEADS
ATTN_Q_HEADS = 8
ATTN_KV_HEADS = 2
ATTN_W = ATTN_Q_HEADS * HEAD_DIM
ATTN_KV_W = ATTN_KV_HEADS * HEAD_DIM
ATTN_IN = ATTN_W + 2 * ATTN_KV_W
ATTN_BLOCK = 128
WINDOW = 128
NUM_BUCKETS = 32
MAX_DISTANCE = 128
D_FF = 4 * D_MODEL
LANE = 128
BF16_ROWS = 16
CHUNK = 64
MIX_TB = 256
MIX_NSEQ = 2
MIX_CHUNK_LEAD = 2
MIX_SEQ_LEAD = 1
SWA_TQ = 512
VMEM_LIMIT = 56 * 1024 * 1024


def _dot(a, b):
    return jnp.dot(a.astype(BF16), b.astype(BF16), preferred_element_type=F32)


def _dot_nt(a, b):
    return lax.dot_general(a.astype(BF16), b.astype(BF16), (((1,), (1,)), ((), ())),
                           preferred_element_type=F32)


def _bf16_parts(x, parts):
    out = []
    r = x
    for p in range(parts):
        hi = r.astype(BF16)
        out.append(hi)
        if p + 1 < parts:
            r = r - hi.astype(F32)
    return out


def _split_dot_r(x, m_bf16, parts):
    rows = x.shape[0]
    t = jnp.dot(jnp.concatenate(_bf16_parts(x, parts), axis=0), m_bf16, preferred_element_type=F32)
    acc = t[:rows]
    for p in range(1, parts):
        acc = acc + t[p * rows:(p + 1) * rows]
    return acc


def _split_dot_l(m_bf16, x, parts):
    cols = x.shape[1]
    t = jnp.dot(m_bf16, jnp.concatenate(_bf16_parts(x, parts), axis=1), preferred_element_type=F32)
    acc = t[:, :cols]
    for p in range(1, parts):
        acc = acc + t[:, p * cols:(p + 1) * cols]
    return acc


def _sigmoid(x):
    return 1.0 / (1.0 + jnp.exp(-x))


def _softplus(x):
    return jnp.maximum(x, 0.0) + jnp.log(1.0 + jnp.exp(-jnp.abs(x)))


def _iota2(shape, dim):
    return lax.broadcasted_iota(jnp.int32, shape, dim)


def _head_ones(n):
    r = _iota2((n, n), 0) // HEAD_DIM
    c = _iota2((n, n), 1) // HEAD_DIM
    return (r == c).astype(BF16)


def _tri_masks():
    t = _iota2((CHUNK, MIX_W), 0)
    s = _iota2((CHUNK, MIX_W), 1) % HEAD_DIM
    return s < t, s <= t


def _run_staggered(gens, starts):
    done = [False] * len(gens)
    rnd = 0
    while not all(done):
        for i, g in enumerate(gens):
            if rnd >= starts[i] and not done[i]:
                try:
                    next(g)
                except StopIteration:
                    done[i] = True
        rnd += 1


MixConsts = collections.namedtuple("MixConsts", "ho ho32 tri cum")


def _mixer_consts():
    n = MIX_W
    r = np.arange(n)[:, None]
    c = np.arange(n)[None, :]
    ho = (r // HEAD_DIM) == (c // HEAD_DIM)
    tri = [r == c, (r // 2) == (c // 2)]
    s = 2
    while s < CHUNK:
        tri.append(((r // (2 * s)) == (c // (2 * s))) & ((r // s) % 2 == 1) & ((c // s) % 2 == 0))
        s *= 2
    cum = np.arange(CHUNK)[None, :] <= np.arange(CHUNK)[:, None]
    return MixConsts(jnp.asarray(ho, BF16), jnp.asarray(ho, F32), jnp.asarray(np.stack(tri), BF16),
                     jnp.asarray(cum, BF16))


def _const_specs():
    fixed = lambda shape: pl.BlockSpec(shape, lambda b, j: (0,) * len(shape), pipeline_mode=pl.Buffered(1))
    ntri = int(math.log2(CHUNK)) + 1
    return [fixed((MIX_W, MIX_W)), fixed((MIX_W, MIX_W)), fixed((ntri, MIX_W, MIX_W)), fixed((CHUNK, CHUNK))]


def _stack_heads(xb, ho):
    return jnp.concatenate([xb] * N_HEADS, axis=0) * ho


def _heads_to_rows(x, hi_half):
    lo = _iota2((x.shape[0], LANE), 1) < HEAD_DIM
    blocks = []
    for h in range(N_HEADS):
        col = x[:, (h // 2) * LANE:(h // 2 + 1) * LANE]
        if (h % 2 == 1) != hi_half:
            col = pltpu.roll(col, HEAD_DIM, axis=1)
        blocks.append(jnp.where(lo != hi_half, col, 0.0))
    return jnp.concatenate(blocks, axis=0)


def _rows_to_heads(z, hi_half):
    c = z.shape[0] // N_HEADS
    lo = _iota2((c, LANE), 1) < HEAD_DIM
    cols = []
    for p in range(N_HEADS // 2):
        even = z[2 * p * c:(2 * p + 1) * c]
        odd = z[(2 * p + 1) * c:(2 * p + 2) * c]
        if hi_half:
            even = pltpu.roll(even, HEAD_DIM, axis=1)
        else:
            odd = pltpu.roll(odd, HEAD_DIM, axis=1)
        cols.append(jnp.where(lo, even, odd))
    return jnp.concatenate(cols, axis=1)


def _tri_inv_stages(a, tri_ref):
    n = a.shape[0]
    t = tri_ref[0] + a * tri_ref[1]
    s = 2
    for lvl in range(2, tri_ref.shape[0]):
        a_off = a * tri_ref[lvl]
        if s % BF16_ROWS:
            inner = jnp.dot(a_off, t, preferred_element_type=F32).astype(BF16)
            yield None
            t = t + jnp.dot(t, inner, preferred_element_type=F32).astype(BF16)
        else:
            odd = [slice(r0 + s, r0 + 2 * s) for r0 in range(0, n, 2 * s)]
            zero = jnp.zeros((s, n), BF16)
            pick = lambda x: jnp.concatenate([x[r] for r in odd], axis=0)
            spread = lambda x: jnp.concatenate(
                [blk for i in range(len(odd)) for blk in (zero, x[i * s:(i + 1) * s])], axis=0)
            inner = spread(jnp.dot(pick(a_off), t, preferred_element_type=F32).astype(BF16))
            yield None
            t = t + spread(jnp.dot(pick(t), inner, preferred_element_type=F32).astype(BF16))
        yield None
        s *= 2
    yield t


def _chain_step(carry, qp, y0, w2, u0, upd_extra, upd_lhs_t, decay, ho32):
    h = carry["h"]
    uy = _dot(jnp.concatenate([w2, qp], axis=0), h)
    yield None
    u = u0 + uy[:CHUNK]
    rows = u if upd_extra is None else jnp.concatenate([u, upd_extra], axis=0)
    carry["h"] = h * decay + jnp.dot(upd_lhs_t, rows.astype(BF16), preferred_element_type=F32) * ho32
    yield uy[CHUNK:] + y0


def _shifted_rows(z_ref, prev_ref, q, c):
    if c == 0:
        return jnp.concatenate([prev_ref[q], z_ref[q, 0:CHUNK, :]], axis=0)
    return z_ref[q, c * CHUNK - 8:(c + 1) * CHUNK, :]


def _mixer_body(chunk_fn, refs, tb, nseq):
    prev_ref, st_ref = refs[-2:]

    @pl.when(pl.program_id(1) == 0)
    def _():
        prev_ref[...] = jnp.zeros_like(prev_ref)
        st_ref[...] = jnp.zeros_like(st_ref)

    assert MIX_CHUNK_LEAD >= 1
    carries = [{} for _ in range(nseq)]
    order = [(c, q) for c in range(tb // CHUNK) for q in range(nseq)]
    gens = [chunk_fn(q, c, carries[q], *refs, tb) for c, q in order]
    _run_staggered(gens, [c * MIX_CHUNK_LEAD + q * MIX_SEQ_LEAD for c, q in order])


def _inproj_kernel(x_ref, lnw_ref, w_ref, za_ref, zb_ref, zc_ref, zba_ref):
    x = x_ref[...]
    h = x * lax.rsqrt(jnp.mean(x * x, axis=-1, keepdims=True) + RMS_EPS) * lnw_ref[...]
    hb = h.astype(BF16)
    o0 = RWKV_IN
    o1 = o0 + GDN_MAIN
    o2 = o1 + ATTN_IN
    za_ref[...] = jnp.dot(hb, w_ref[:, 0:o0], preferred_element_type=F32)
    zb_ref[...] = jnp.dot(hb, w_ref[:, o0:o1], preferred_element_type=F32)
    zc_ref[...] = jnp.dot(hb, w_ref[:, o1:o2], preferred_element_type=F32)
    zba_ref[...] = jnp.dot(hb, w_ref[:, o2:o2 + LANE], preferred_element_type=F32)


def _inproj(x2, lnw, w, tm):
    t = x2.shape[0]
    n = w.shape[1]
    row = lambda i: (i, 0)
    fixed = lambda i: (0, 0)
    return pl.pallas_call(
        _inproj_kernel,
        grid=(t // tm,),
        in_specs=[pl.BlockSpec((tm, D_MODEL), row),
                  pl.BlockSpec((1, D_MODEL), fixed),
                  pl.BlockSpec((D_MODEL, n), fixed, pipeline_mode=pl.Buffered(1))],
        out_specs=[pl.BlockSpec((tm, RWKV_IN), row),
                   pl.BlockSpec((tm, GDN_MAIN), row),
                   pl.BlockSpec((tm, ATTN_IN), row),
                   pl.BlockSpec((tm, LANE), row)],
        out_shape=[jax.ShapeDtypeStruct((t, RWKV_IN), F32),
                   jax.ShapeDtypeStruct((t, GDN_MAIN), F32),
                   jax.ShapeDtypeStruct((t, ATTN_IN), F32),
                   jax.ShapeDtypeStruct((t, LANE), F32)],
        compiler_params=pltpu.CompilerParams(dimension_semantics=("parallel",),
                                             vmem_limit_bytes=VMEM_LIMIT),
        name="inproj",
    )(x2, lnw, w)


def _rwkv_chunk(q, c, carry, z_ref, mu_ref, w0_ref, wup_ref, a0_ref, aup_ref, gup_ref, kk_ref, ka_ref, rk_ref,
                lnw_ref, lnb_ref, ho_ref, ho32_ref, tri_ref, cum_ref, o_ref, prev_ref, st_ref, tb):
    w = MIX_W
    nchunk = tb // CHUNK
    ext = _shifted_rows(z_ref, prev_ref, q, c)
    z = ext[8:]
    zprev = pltpu.roll(ext, 1, axis=0)[8:]
    zz = z + (zprev - z) * mu_ref[...]
    r = zz[:, 0:w]
    k = zz[:, w:2 * w]
    v = zz[:, 2 * w:3 * w]
    dwa = zz[:, 3 * w:3 * w + LANE]
    dg = zz[:, 3 * w + LANE:]
    yield
    log_w = -_softplus(-(w0_ref[...] + _dot(jnp.tanh(dwa), wup_ref[...]))) - 0.5
    ld = -jnp.exp(log_w)
    iclr =_sigmoid(a0_ref[...] + _dot(dwa, aup_ref[...]))
    gate = _dot(_sigmoid(dg), gup_ref[...])
    yield
    ho = ho_ref[...]
    kkv = k * kk_ref[...]
    k2 = k * (1.0 + (iclr - 1.0) * ka_ref[...])
    sums = _split_dot_r(jnp.concatenate([kkv * kkv, r * k2 * rk_ref[...]], axis=0), ho, 1)
    kkn = kkv * lax.rsqrt(sums[:CHUNK] + L2_EPS)
    bonus = sums[CHUNK:] * v
    b = kkn * iclr
    yield
    cum = _split_dot_l(cum_ref[...], ld, 2)
    e_neg = jnp.exp(-cum)
    at = (-kkn) * jnp.exp(cum - ld)
    atb = at.astype(BF16)
    bt = (b * e_neg).astype(BF16)
    kt = (k2 * e_neg).astype(BF16)
    rt = r * jnp.exp(cum)
    at_rows =_heads_to_rows(at, False)
    v_rows = _heads_to_rows(v, True).astype(BF16)
    yield
    strict, incl = _tri_masks()
    g = lax.dot_general(jnp.concatenate([atb, rt.astype(BF16)], axis=0),
                        jnp.concatenate([_stack_heads(bt, ho), _stack_heads(kt, ho)], axis=0),
                        (((1,), (1,)), ((), ())), preferred_element_type=F32)
    yield
    a_ab = _stack_heads(jnp.where(strict, g[:CHUNK, :w], 0.0).astype(BF16), ho)
    a_ak = _stack_heads(jnp.where(strict, g[:CHUNK, w:], 0.0).astype(BF16), ho)
    a_rb = _stack_heads(jnp.where(incl, g[CHUNK:, :w], 0.0).astype(BF16), ho)
    a_rk = _stack_heads(jnp.where(incl, g[CHUNK:, w:], 0.0).astype(BF16), ho)
    yield
    t_inv = None
    for t_inv in _tri_inv_stages(a_ab, tri_ref):
        if t_inv is None:
            yield
    pv = jnp.dot(jnp.concatenate([a_ak, a_rk], axis=0), v_rows, preferred_element_type=F32)
    yield
    wu = jnp.dot(t_inv, (at_rows + pv[:w]).astype(BF16), preferred_element_type=F32)
    yield
    qy = jnp.dot(a_rb, wu.astype(BF16), preferred_element_type=F32) + pv[w:]
    yield
    w2 = _rows_to_heads(wu, False)
    u0 = _rows_to_heads(wu, True)
    qp = rt + _rows_to_heads(qy, False)
    y0 = _rows_to_heads(qy, True)
    last = cum[CHUNK - 1:CHUNK, :]
    to_end = jnp.exp(last - cum)
    upd_lhs_t = jnp.concatenate([b * to_end, k2 * to_end], axis=0).T.astype(BF16)
    decay_col = jnp.broadcast_to(jnp.exp(last), (8, w)).T[:, 0:1]
    yield
    if c == 0:
        carry["h"] = st_ref[q]
    y = None
    for y in _chain_step(carry, qp, y0, w2, u0, v, upd_lhs_t, decay_col, ho32_ref[...]):
        if y is None:
            yield
    if c == nchunk - 1:
        st_ref[q] = carry["h"]
        prev_ref[q] = z_ref[q, tb - 8:tb, :]
    yield
    inv_n = 1.0 / HEAD_DIM
    mean = _split_dot_r(y, ho, 1) * inv_n
    d = y - mean
    yield
    var = _split_dot_r(d * d, ho, 1) * inv_n
    yn = d * lax.rsqrt(var + RWKV_GN_EPS) * lnw_ref[...] + lnb_ref[...]
    o_ref[q, c * CHUNK:(c + 1) * CHUNK, :] = ((yn + bonus) * gate).astype(o_ref.dtype)


def _rwkv_kernel(*refs, tb, nseq):
    _mixer_body(_rwkv_chunk, refs, tb, nseq)


def _rwkv(za, mu, w0, wup, a0, aup, gup, k_k, k_a, r_k, lnw, lnb, consts, tb, nseq):
    bsz, s, _ = za.shape
    vec = lambda n: pl.BlockSpec((1, n), lambda b, j: (0, 0))
    mat = lambda m, n: pl.BlockSpec((m, n), lambda b, j: (0, 0))
    return pl.pallas_call(
        functools.partial(_rwkv_kernel, tb=tb, nseq=nseq),
        grid=(bsz // nseq, s // tb),
        in_specs=[pl.BlockSpec((nseq, tb, RWKV_IN), lambda b, j: (b, j, 0)),
                  vec(RWKV_IN), vec(MIX_W), mat(LANE, MIX_W), vec(MIX_W), mat(LANE, MIX_W),
                  mat(RWKV_GATE_RANK, MIX_W), vec(MIX_W), vec(MIX_W), vec(MIX_W), vec(MIX_W), vec(MIX_W)]
        + _const_specs(),
        out_specs=pl.BlockSpec((nseq, tb, MIX_W), lambda b, j: (b, j, 0)),
        out_shape=jax.ShapeDtypeStruct((bsz, s, MIX_W), BF16),
        scratch_shapes=[pltpu.VMEM((nseq, 8, RWKV_IN), F32), pltpu.VMEM((nseq, MIX_W, MIX_W), F32)],
        compiler_params=pltpu.CompilerParams(dimension_semantics=("parallel", "arbitrary"),
                                             vmem_limit_bytes=VMEM_LIMIT),
        name="rwkv7",
    )(za, mu, w0, wup, a0, aup, gup, k_k, k_a, r_k, lnw, lnb, *consts)


def _gdn_chunk(q, c, carry, z_ref, ba_ref, cw_ref, alog_ref, dtb_ref, nw_ref, ho_ref, ho32_ref, tri_ref, cum_ref,
               o_ref, prev_ref, st_ref, tb):
    w = MIX_W
    nchunk = tb // CHUNK
    ext = _shifted_rows(z_ref, prev_ref, q, c)
    x = ext[:, :3 * w]
    gate = ext[8:, 3 * w:]
    acc = x[8:] * cw_ref[GDN_CONV - 1:GDN_CONV, :]
    for s in range(1, GDN_CONV):
        acc = acc + pltpu.roll(x, s, axis=0)[8:] * cw_ref[GDN_CONV - 1 - s:GDN_CONV - s, :]
    yield
    qkv = acc * _sigmoid(acc)
    qh = qkv[:, :w]
    k = qkv[:, w:2 * w]
    v = qkv[:, 2 * w:]
    ho = ho_ref[...]
    sums = _split_dot_r(jnp.concatenate([qh * qh, k * k], axis=0), ho, 1)
    qn = qh * lax.rsqrt(sums[:CHUNK] + L2_EPS) * (HEAD_DIM ** -0.5)
    kn = k * lax.rsqrt(sums[CHUNK:] + L2_EPS)
    yield
    ba = ba_ref[q, c * CHUNK:(c + 1) * CHUNK, :]
    er = _iota2((LANE, 2 * w), 0)
    ec = _iota2((LANE, 2 * w), 1)
    spread = (er == (ec % w) // HEAD_DIM + N_HEADS * (ec // w)).astype(BF16)
    ba_full = _split_dot_r(ba, spread, 2)
    beta = _sigmoid(ba_full[:, :w])
    g = -jnp.exp(alog_ref[...]) * _softplus(ba_full[:, w:] + dtb_ref[...])
    yield
    gc = _split_dot_l(cum_ref[...], g, 3)
    gt = gc.T
    eg = jnp.exp(gc)
    kb = kn * beta
    qe = qn * eg
    kv_rows = (_heads_to_rows(kb * eg, False) + _heads_to_rows(v * beta, True)).astype(BF16)
    yield
    strict, incl = _tri_masks()
    s1 = lax.dot_general(jnp.concatenate([kb, qn], axis=0).astype(BF16), _stack_heads(kn.astype(BF16), ho),
                         (((1,), (1,)), ((), ())), preferred_element_type=F32)
    yield
    grow = jnp.concatenate([gt[h * HEAD_DIM:h * HEAD_DIM + 1, :] for h in range(N_HEADS)], axis=1)
    dec = jnp.exp(jnp.where(incl, gc - grow, -jnp.inf))
    lower = _stack_heads(jnp.where(strict, -(s1[:CHUNK] * dec), 0.0).astype(BF16), ho)
    intra = _stack_heads((s1[CHUNK:] * dec).astype(BF16), ho)
    yield
    t_inv = None
    for t_inv in _tri_inv_stages(lower, tri_ref):
        if t_inv is None:
            yield
    wu = jnp.dot(t_inv, kv_rows, preferred_element_type=F32)
    yield
    qy = jnp.dot(intra, wu.astype(BF16), preferred_element_type=F32)
    yield
    w2 = -_rows_to_heads(wu, False)
    u0 = _rows_to_heads(wu, True)
    qp = qe - _rows_to_heads(qy, False)
    y0 = _rows_to_heads(qy, True)
    last = gc[CHUNK - 1:CHUNK, :]
    kdec_t = (kn * jnp.exp(last - gc)).T.astype(BF16)
    yield
    if c == 0:
        carry["h"] = st_ref[q]
    o = None
    for o in _chain_step(carry, qp, y0, w2, u0, None, kdec_t, jnp.exp(last), ho32_ref[...]):
        if o is None:
            yield
    if c == nchunk - 1:
        st_ref[q] = carry["h"]
        prev_ref[q] = z_ref[q, tb - 8:tb, :]
    yield
    ms = _split_dot_r(o * o, ho, 1) * (1.0 / HEAD_DIM)
    on = o * lax.rsqrt(ms + RMS_EPS) * nw_ref[...]
    o_ref[q, c * CHUNK:(c + 1) * CHUNK, :] = (on * (gate * _sigmoid(gate))).astype(o_ref.dtype)


def _gdn_kernel(*refs, tb, nseq):
    _mixer_body(_gdn_chunk, refs, tb, nseq)


def _gdn(zb, zba, cw, alog, dtb, nw, consts, tb, nseq):
    bsz, s, _ = zb.shape
    vec = lambda n: pl.BlockSpec((1, n), lambda b, j: (0, 0))
    return pl.pallas_call(
        functools.partial(_gdn_kernel, tb=tb, nseq=nseq),
        grid=(bsz // nseq, s // tb),
        in_specs=[pl.BlockSpec((nseq, tb, GDN_MAIN), lambda b, j: (b, j, 0)),
                  pl.BlockSpec((nseq, tb, LANE), lambda b, j: (b, j, 0)),
                  pl.BlockSpec((GDN_CONV, 3 * MIX_W), lambda b, j: (0, 0)),
                  vec(MIX_W), vec(MIX_W), vec(MIX_W)] + _const_specs(),
        out_specs=pl.BlockSpec((nseq, tb, MIX_W), lambda b, j: (b, j, 0)),
        out_shape=jax.ShapeDtypeStruct((bsz, s, MIX_W), BF16),
        scratch_shapes=[pltpu.VMEM((nseq, 8, GDN_MAIN), F32), pltpu.VMEM((nseq, MIX_W, MIX_W), F32)],
        compiler_params=pltpu.CompilerParams(dimension_semantics=("parallel", "arbitrary"),
                                             vmem_limit_bytes=VMEM_LIMIT),
        name="gdn",
    )(zb, zba, cw, alog, dtb, nw, *consts)


def _t5_bucket_table():
    L = ATTN_BLOCK
    i = np.arange(L)[None, :]
    j = np.arange(2 * L)[:, None]
    dist = np.maximum(i + L - j, 0)
    max_exact = NUM_BUCKETS // 2
    nf = np.maximum(dist, max_exact).astype(np.float32)
    large = max_exact + (np.log(nf / max_exact) / math.log(MAX_DISTANCE / max_exact)
                         * (NUM_BUCKETS - max_exact)).astype(np.int32)
    large = np.minimum(large, NUM_BUCKETS - 1)
    return np.where(dist < max_exact, dist, large).astype(np.int32)


def _bias_kernel(bkt_ref, rel_ref, o_ref):
    L = ATTN_BLOCK
    grp = ATTN_Q_HEADS // ATTN_KV_HEADS
    bkt = bkt_ref[...]
    kj = _iota2(bkt.shape, 0)
    dist = _iota2(bkt.shape, 1) + L - kj
    in_window = (dist >= 0) & (dist < WINDOW)
    for h in range(ATTN_Q_HEADS):
        acc = jnp.zeros(bkt.shape, F32)
        for b in range(NUM_BUCKETS):
            acc = jnp.where(bkt == b, rel_ref[b, h], acc)
        cols = slice((h % grp) * L, (h % grp + 1) * L)
        o_ref[1, h // grp, :, cols] = jnp.where(in_window, acc, -jnp.inf)
        o_ref[0, h // grp, :, cols] = jnp.where(in_window & (kj >= L), acc, -jnp.inf)


def _bias_band(rel_bias):
    L = ATTN_BLOCK
    shape = (2, ATTN_KV_HEADS, 2 * L, (ATTN_Q_HEADS // ATTN_KV_HEADS) * L)
    return pl.pallas_call(
        _bias_kernel,
        in_specs=[pl.BlockSpec((2 * L, L), lambda: (0, 0)),
                  pl.BlockSpec(memory_space=pltpu.SMEM)],
        out_specs=pl.BlockSpec(shape, lambda: (0, 0, 0, 0)),
        out_shape=jax.ShapeDtypeStruct(shape, F32),
        name="t5_bias",
    )(jnp.asarray(_t5_bucket_table()), rel_bias)


def _swa_kernel(z_ref, bias0_ref, bias_ref, qnw_ref, knw_ref, sink_ref, o_ref, kprev_ref, vprev_ref, *, nsub):
    @pl.when(pl.program_id(1) == 0)
    def _():
        kprev_ref[...] = jnp.zeros_like(kprev_ref)
        vprev_ref[...] = jnp.zeros_like(vprev_ref)

    L = ATTN_BLOCK
    tq = nsub * L
    grp = ATTN_Q_HEADS // ATTN_KV_HEADS
    ngroups = ATTN_W // LANE
    z = z_ref[0]
    k = z[:, ATTN_W:ATTN_W + ATTN_KV_W]
    v = z[:, ATTN_W + ATTN_KV_W:]
    qg = [z[:, p * LANE:(p + 1) * LANE] for p in range(ngroups)]
    sq = _split_dot_r(jnp.concatenate([x * x for x in qg] + [k * k], axis=0), _head_ones(LANE), 2)
    inv_d = 1.0 / HEAD_DIM
    qscale = qnw_ref[...] * (HEAD_DIM ** -0.5)
    qn = [qg[p] * lax.rsqrt(sq[p * tq:(p + 1) * tq] * inv_d + RMS_EPS) * qscale for p in range(ngroups)]
    kn = k * lax.rsqrt(sq[ngroups * tq:] * inv_d + RMS_EPS) * knw_ref[...]
    kall = jnp.concatenate([kprev_ref[...], kn], axis=0)
    vall = jnp.concatenate([vprev_ref[...], v], axis=0)
    kprev_ref[...] = kn[tq - L:]
    vprev_ref[...] = v[tq - L:]
    kroll = pltpu.roll(kall, HEAD_DIM, axis=1)
    vroll = pltpu.roll(vall, HEAD_DIM, axis=1)
    lo_k = _iota2(kall.shape, 1) < HEAD_DIM
    lo_q = _iota2((L, LANE), 1) < HEAD_DIM
    kd = [jnp.where(lo_k, kall, kroll), jnp.where(lo_k, kroll, kall)]
    vdt = [jnp.where(lo_k, vall, vroll).T, jnp.where(lo_k, vroll, vall).T]

    qlane_head = _iota2((1, grp * L), 1) // L

    def block_head(i, j):
        rows = slice(i * L, (i + 1) * L)
        keys = slice(i * L, (i + 2) * L)
        qs = []
        for g in range(grp):
            hq = grp * j + g
            qrow = qn[hq // 2][rows]
            qs.append(jnp.where(lo_q, qrow, 0.0) if hq % 2 == 0 else jnp.where(lo_q, 0.0, qrow))
        bias = bias0_ref[0, j] if i == 0 else bias_ref[0, j]
        sc = _dot_nt(kd[j][keys], jnp.concatenate(qs, axis=0)) + bias
        yield
        sink = jnp.zeros((1, grp * L), F32)
        for g in range(grp):
            sink = jnp.where(qlane_head == g, sink_ref[grp * j + g], sink)
        m = jnp.maximum(jnp.max(sc, axis=0, keepdims=True), sink)
        pe = jnp.exp(sc - m)
        den = jnp.sum(pe, axis=0, keepdims=True) + jnp.exp(sink - m)
        yield
        ot = _dot(vdt[j][:, keys], pe) / den
        yield
        o = ot.T
        for g in range(0, grp, 2):
            o_ref[0, rows, (grp * j + g) // 2 * LANE:((grp * j + g) // 2 + 1) * LANE] = jnp.where(
                lo_q, o[g * L:(g + 1) * L], o[(g + 1) * L:(g + 2) * L]).astype(o_ref.dtype)

    gens = [block_head(i, j) for i in range(nsub) for j in range(ATTN_KV_HEADS)]
    _run_staggered(gens, list(range(len(gens))))


def _swa(zc, bias, qnw, knw, sinks, tq):
    bsz, s, _ = zc.shape
    L = ATTN_BLOCK
    bias_block = (1,) + bias.shape[1:]
    return pl.pallas_call(
        functools.partial(_swa_kernel, nsub=tq // L),
        grid=(bsz, s // tq),
        in_specs=[pl.BlockSpec((1, tq, ATTN_IN), lambda b, n: (b, n, 0)),
                  pl.BlockSpec(bias_block, lambda b, n: (jnp.minimum(n, 1), 0, 0, 0)),
                  pl.BlockSpec(bias_block, lambda b, n: (1, 0, 0, 0)),
                  pl.BlockSpec((1, LANE), lambda b, n: (0, 0)),
                  pl.BlockSpec((1, LANE), lambda b, n: (0, 0)),
                  pl.BlockSpec(memory_space=pltpu.SMEM)],
        out_specs=pl.BlockSpec((1, tq, ATTN_W), lambda b, n: (b, n, 0)),
        out_shape=jax.ShapeDtypeStruct((bsz, s, ATTN_W), BF16),
        scratch_shapes=[pltpu.VMEM((L, ATTN_KV_W), F32), pltpu.VMEM((L, ATTN_KV_W), F32)],
        compiler_params=pltpu.CompilerParams(dimension_semantics=("parallel", "arbitrary"),
                                             vmem_limit_bytes=VMEM_LIMIT),
        name="swa",
    )(zc, bias, bias, qnw, knw, sinks)


def _mlp_kernel(x_ref, ya_ref, yb_ref, yc_ref, wo_ref, ln2_ref, w1_ref, w2_ref, o_ref, *, tf):
    y = (jnp.dot(ya_ref[...], wo_ref[0:MIX_W, :], preferred_element_type=F32)
         + jnp.dot(yb_ref[...], wo_ref[MIX_W:2 * MIX_W, :], preferred_element_type=F32)
         + jnp.dot(yc_ref[...], wo_ref[2 * MIX_W:, :], preferred_element_type=F32))
    x1 = x_ref[...] + y
    h = (x1 * lax.rsqrt(jnp.mean(x1 * x1, axis=-1, keepdims=True) + RMS_EPS) * ln2_ref[...]).astype(BF16)
    acc = x1
    for j in range(D_FF // tf):
        u = jnp.dot(h, w1_ref[:, j * tf:(j + 1) * tf], preferred_element_type=F32)
        u = jnp.square(jnp.maximum(u, 0.0))
        acc = acc + jnp.dot(u.astype(BF16), w2_ref[j * tf:(j + 1) * tf, :], preferred_element_type=F32)
    o_ref[...] = acc


def _out_mlp(x2, ya, yb, yc, wo, ln2, w1, w2, tm, tf):
    t = x2.shape[0]
    row = lambda i: (i, 0)
    fixed = lambda shape: pl.BlockSpec(shape, lambda i: (0, 0), pipeline_mode=pl.Buffered(1))
    return pl.pallas_call(
        functools.partial(_mlp_kernel, tf=tf),
        grid=(t // tm,),
        in_specs=[pl.BlockSpec((tm, D_MODEL), row),
                  pl.BlockSpec((tm, MIX_W), row),
                  pl.BlockSpec((tm, MIX_W), row),
                  pl.BlockSpec((tm, ATTN_W), row),
                  fixed((D_MODEL, D_MODEL)),
                  fixed((1, D_MODEL)),
                  fixed((D_MODEL, D_FF)),
                  fixed((D_FF, D_MODEL))],
        out_specs=pl.BlockSpec((tm, D_MODEL), row),
        out_shape=jax.ShapeDtypeStruct((t, D_MODEL), F32),
        compiler_params=pltpu.CompilerParams(dimension_semantics=("parallel",),
                                             vmem_limit_bytes=VMEM_LIMIT),
        name="out_mlp",
    )(x2, ya, yb, yc, wo, ln2, w1, w2)


def _pad_rows(m, rows, offset):
    return jnp.zeros((rows, m.shape[1]), m.dtype).at[offset:offset + m.shape[0]].set(m)


def _per_head(p):
    return jnp.repeat(p, HEAD_DIM)[None, :]


def _layer(x2, bsz, s, bias, consts, p):
    t = bsz * s
    tb = min(MIX_TB, s)
    nseq = MIX_NSEQ if bsz % MIX_NSEQ == 0 else 1
    w_in = p["w_in"]
    o_b = RWKV_IN
    o_ba = o_b + GDN_MAIN
    o_c = o_b + GDN_IN
    w_re = jnp.concatenate(
        [w_in[:, :o_ba], w_in[:, o_c:], w_in[:, o_ba:o_c],
         jnp.zeros((D_MODEL, LANE - 2 * N_HEADS), w_in.dtype)], axis=1).astype(BF16)
    za, zb, zc, zba = _inproj(x2, p["ln1_w"][None, :], w_re, min(1024, t))

    row = lambda a: a[None, :]
    ya = _rwkv(za.reshape(bsz, s, RWKV_IN), row(p["rwkv_mu"]), row(p["rwkv_w0"]),
               _pad_rows(p["rwkv_w_up"], LANE, 0).astype(BF16), row(p["rwkv_a0"]),
               _pad_rows(p["rwkv_a_up"], LANE, RWKV_DECAY_RANK).astype(BF16),
               p["rwkv_g_up"].astype(BF16), row(p["rwkv_k_k"]), row(p["rwkv_k_a"]),
               p["rwkv_r_k"].reshape(1, MIX_W), row(p["rwkv_lnx_w"]), row(p["rwkv_lnx_b"]),
               consts, tb, nseq)
    yb = _gdn(zb.reshape(bsz, s, GDN_MAIN), zba.reshape(bsz, s, LANE), p["gdn_conv_w"],
              _per_head(p["gdn_a_log"]), _per_head(p["gdn_dt_bias"]),
              jnp.tile(p["gdn_norm_w"], N_HEADS)[None, :], consts, tb, nseq)
    yc = _swa(zc.reshape(bsz, s, ATTN_IN), bias, jnp.tile(p["attn_q_norm_w"], 2)[None, :],
              jnp.tile(p["attn_k_norm_w"], 2)[None, :], p["attn_sinks"], min(SWA_TQ, s))
    return _out_mlp(x2, ya.reshape(t, MIX_W), yb.reshape(t, MIX_W), yc.reshape(t, ATTN_W),
                    p["w_out"].astype(BF16), p["ln2_w"][None, :], p["w_ff1"].astype(BF16),
                    p["w_ff2"].astype(BF16), min(1024, t), min(1024, D_FF))


_LAYER_PARAMS = ("ln1_w", "w_in", "rwkv_mu", "rwkv_w0", "rwkv_w_up", "rwkv_a0", "rwkv_a_up", "rwkv_g_up",
                 "rwkv_k_k", "rwkv_k_a", "rwkv_r_k", "rwkv_lnx_w", "rwkv_lnx_b", "gdn_conv_w", "gdn_a_log",
                 "gdn_dt_bias", "gdn_norm_w", "attn_q_norm_w", "attn_k_norm_w", "attn_sinks",
                 "w_out", "ln2_w", "w_ff1", "w_ff2")


def kernel(x, ln1_w, w_in, rwkv_mu, rwkv_w0, rwkv_w_up, rwkv_a0, rwkv_a_up, rwkv_g_up, rwkv_k_k, rwkv_k_a,
           rwkv_r_k, rwkv_lnx_w, rwkv_lnx_b, gdn_conv_w, gdn_a_log, gdn_dt_bias, gdn_norm_w, attn_q_norm_w,
           attn_k_norm_w, attn_sinks, rel_bias, w_out, ln2_w, w_ff1, w_ff2):
    stacked = dict(ln1_w=ln1_w, w_in=w_in, rwkv_mu=rwkv_mu, rwkv_w0=rwkv_w0, rwkv_w_up=rwkv_w_up,
                   rwkv_a0=rwkv_a0, rwkv_a_up=rwkv_a_up, rwkv_g_up=rwkv_g_up, rwkv_k_k=rwkv_k_k,
                   rwkv_k_a=rwkv_k_a, rwkv_r_k=rwkv_r_k, rwkv_lnx_w=rwkv_lnx_w, rwkv_lnx_b=rwkv_lnx_b,
                   gdn_conv_w=gdn_conv_w, gdn_a_log=gdn_a_log, gdn_dt_bias=gdn_dt_bias, gdn_norm_w=gdn_norm_w,
                   attn_q_norm_w=attn_q_norm_w, attn_k_norm_w=attn_k_norm_w, attn_sinks=attn_sinks,
                   w_out=w_out, ln2_w=ln2_w, w_ff1=w_ff1, w_ff2=w_ff2)
    bsz, s, _ = x.shape
    bias = _bias_band(rel_bias)
    consts = _mixer_consts()
    x2 = x.reshape(bsz * s, D_MODEL)
    for l in range(w_in.shape[0]):
        x2 = _layer(x2, bsz, s, bias, consts, {name: stacked[name][l] for name in _LAYER_PARAMS})
    return x2.reshape(bsz, s, D_MODEL)
```

```python
import collections
import functools
import math

import numpy as np
import jax
import jax.numpy as jnp
from jax import lax
from jax.experimental import pallas as pl
from jax.experimental.pallas import tpu as pltpu

F32 = jnp.float32
BF16 = jnp.bfloat16

D_MODEL = 1024
DEPTH = 4
HEAD_DIM = 64
RMS_EPS = 1e-6
L2_EPS = 1e-6
N_HEADS = 4
MIX_W = N_HEADS * HEAD_DIM
RWKV_DECAY_RANK = 64
RWKV_ICLR_RANK = 64
RWKV_GATE_RANK = 128
RWKV_GN_EPS = 64e-5
RWKV_IN = 3 * MIX_W + RWKV_DECAY_RANK + RWKV_ICLR_RANK + RWKV_GATE_RANK
GDN_CONV = 4
GDN_MAIN = 4 * MIX_W
GDN_IN = GDN_MAIN + 2 * N_HEADS
ATTN_Q_HEADS = 8
ATTN_KV_HEADS = 2
ATTN_W = ATTN_Q_HEADS * HEAD_DIM
ATTN_KV_W = ATTN_KV_HEADS * HEAD_DIM
ATTN_IN = ATTN_W + 2 * ATTN_KV_W
ATTN_BLOCK = 128
WINDOW = 128
NUM_BUCKETS = 32
MAX_DISTANCE = 128
D_FF = 4 * D_MODEL
LANE = 128
BF16_ROWS = 16
CHUNK = 64
MIX_TB = 256
MIX_NSEQ = 2
MIX_CHUNK_LEAD = 2
MIX_SEQ_LEAD = 1
SWA_TQ = 512
VMEM_LIMIT = 56 * 1024 * 1024


def _dot(a, b):
    return jnp.dot(a.astype(BF16), b.astype(BF16), preferred_element_type=F32)


def _dot_nt(a, b):
    return lax.dot_general(a.astype(BF16), b.astype(BF16), (((1,), (1,)), ((), ())),
                           preferred_element_type=F32)


def _bf16_parts(x, parts):
    out = []
    r = x
    for p in range(parts):
        hi = r.astype(BF16)
        out.append(hi)
        if p + 1 < parts:
            r = r - hi.astype(F32)
    return out


def _split_dot_r(x, m_bf16, parts):
    rows = x.shape[0]
    t = jnp.dot(jnp.concatenate(_bf16_parts(x, parts), axis=0), m_bf16, preferred_element_type=F32)
    acc = t[:rows]
    for p in range(1, parts):
        acc = acc + t[p * rows:(p + 1) * rows]
    return acc


def _split_dot_l(m_bf16, x, parts):
    cols = x.shape[1]
    t = jnp.dot(m_bf16, jnp.concatenate(_bf16_parts(x, parts), axis=1), preferred_element_type=F32)
    acc = t[:, :cols]
    for p in range(1, parts):
        acc = acc + t[:, p * cols:(p + 1) * cols]
    return acc


def _sigmoid(x):
    return 1.0 / (1.0 + jnp.exp(-x))


def _softplus(x):
    return jnp.maximum(x, 0.0) + jnp.log(1.0 + jnp.exp(-jnp.abs(x)))


def _iota2(shape, dim):
    return lax.broadcasted_iota(jnp.int32, shape, dim)


def _head_ones(n):
    r = _iota2((n, n), 0) // HEAD_DIM
    c = _iota2((n, n), 1) // HEAD_DIM
    return (r == c).astype(BF16)


def _tri_masks():
    t = _iota2((CHUNK, MIX_W), 0)
    s = _iota2((CHUNK, MIX_W), 1) % HEAD_DIM
    return s < t, s <= t


def _run_staggered(gens, starts):
    done = [False] * len(gens)
    rnd = 0
    while not all(done):
        for i, g in enumerate(gens):
            if rnd >= starts[i] and not done[i]:
                try:
                    next(g)
                except StopIteration:
                    done[i] = True
        rnd += 1


MixConsts = collections.namedtuple("MixConsts", "ho ho32 tri cum")


def _mixer_consts():
    n = MIX_W
    r = np.arange(n)[:, None]
    c = np.arange(n)[None, :]
    ho = (r // HEAD_DIM) == (c // HEAD_DIM)
    tri = [r == c, (r // 2) == (c // 2)]
    s = 2
    while s < CHUNK:
        tri.append(((r // (2 * s)) == (c // (2 * s))) & ((r // s) % 2 == 1) & ((c // s) % 2 == 0))
        s *= 2
    cum = np.arange(CHUNK)[None, :] <= np.arange(CHUNK)[:, None]
    return MixConsts(jnp.asarray(ho, BF16), jnp.asarray(ho, F32), jnp.asarray(np.stack(tri), BF16),
                     jnp.asarray(cum, BF16))


def _const_specs():
    fixed = lambda shape: pl.BlockSpec(shape, lambda b, j: (0,) * len(shape), pipeline_mode=pl.Buffered(1))
    ntri = int(math.log2(CHUNK)) + 1
    return [fixed((MIX_W, MIX_W)), fixed((MIX_W, MIX_W)), fixed((ntri, MIX_W, MIX_W)), fixed((CHUNK, CHUNK))]


def _stack_heads(xb, ho):
    return jnp.concatenate([xb] * N_HEADS, axis=0) * ho


def _heads_to_rows(x, hi_half):
    lo = _iota2((x.shape[0], LANE), 1) < HEAD_DIM
    blocks = []
    for h in range(N_HEADS):
        col = x[:, (h // 2) * LANE:(h // 2 + 1) * LANE]
        if (h % 2 == 1) != hi_half:
            col = pltpu.roll(col, HEAD_DIM, axis=1)
        blocks.append(jnp.where(lo != hi_half, col, 0.0))
    return jnp.concatenate(blocks, axis=0)


def _rows_to_heads(z, hi_half):
    c = z.shape[0] // N_HEADS
    lo = _iota2((c, LANE), 1) < HEAD_DIM
    cols = []
    for p in range(N_HEADS // 2):
        even = z[2 * p * c:(2 * p + 1) * c]
        odd = z[(2 * p + 1) * c:(2 * p + 2) * c]
        if hi_half:
            even = pltpu.roll(even, HEAD_DIM, axis=1)
        else:
            odd = pltpu.roll(odd, HEAD_DIM, axis=1)
        cols.append(jnp.where(lo, even, odd))
    return jnp.concatenate(cols, axis=1)


def _tri_inv_stages(a, tri_ref):
    n = a.shape[0]
    t = tri_ref[0] + a * tri_ref[1]
    s = 2
    for lvl in range(2, tri_ref.shape[0]):
        a_off = a * tri_ref[lvl]
        if s % BF16_ROWS:
            inner = jnp.dot(a_off, t, preferred_element_type=F32).astype(BF16)
            yield None
            t = t + jnp.dot(t, inner, preferred_element_type=F32).astype(BF16)
        else:
            odd = [slice(r0 + s, r0 + 2 * s) for r0 in range(0, n, 2 * s)]
            zero = jnp.zeros((s, n), BF16)
            pick = lambda x: jnp.concatenate([x[r] for r in odd], axis=0)
            spread = lambda x: jnp.concatenate(
                [blk for i in range(len(odd)) for blk in (zero, x[i * s:(i + 1) * s])], axis=0)
            inner = spread(jnp.dot(pick(a_off), t, preferred_element_type=F32).astype(BF16))
            yield None
            t = t + spread(jnp.dot(pick(t), inner, preferred_element_type=F32).astype(BF16))
        yield None
        s *= 2
    yield t


def _chain_step(carry, qp, y0, w2, u0, upd_extra, upd_lhs_t, decay, ho32):
    h = carry["h"]
    uy = _dot(jnp.concatenate([w2, qp], axis=0), h)
    yield None
    u = u0 + uy[:CHUNK]
    rows = u if upd_extra is None else jnp.concatenate([u, upd_extra], axis=0)
    carry["h"] = h * decay + jnp.dot(upd_lhs_t, rows.astype(BF16), preferred_element_type=F32) * ho32
    yield uy[CHUNK:] + y0


def _shifted_rows(z_ref, prev_ref, q, c):
    if c == 0:
        return jnp.concatenate([prev_ref[q], z_ref[q, 0:CHUNK, :]], axis=0)
    return z_ref[q, c * CHUNK - 8:(c + 1) * CHUNK, :]


def _mixer_body(chunk_fn, refs, tb, nseq):
    prev_ref, st_ref = refs[-2:]

    @pl.when(pl.program_id(1) == 0)
    def _():
        prev_ref[...] = jnp.zeros_like(prev_ref)
        st_ref[...] = jnp.zeros_like(st_ref)

    assert MIX_CHUNK_LEAD >= 1
    carries = [{} for _ in range(nseq)]
    order = [(c, q) for c in range(tb // CHUNK) for q in range(nseq)]
    gens = [chunk_fn(q, c, carries[q], *refs, tb) for c, q in order]
    _run_staggered(gens, [c * MIX_CHUNK_LEAD + q * MIX_SEQ_LEAD for c, q in order])


def _inproj_kernel(x_ref, lnw_ref, w_ref, za_ref, zb_ref, zc_ref, zba_ref):
    x = x_ref[...]
    h = x * lax.rsqrt(jnp.mean(x * x, axis=-1, keepdims=True) + RMS_EPS) * lnw_ref[...]
    hb = h.astype(BF16)
    o0 = RWKV_IN
    o1 = o0 + GDN_MAIN
    o2 = o1 + ATTN_IN
    za_ref[...] = jnp.dot(hb, w_ref[:, 0:o0], preferred_element_type=F32)
    zb_ref[...] = jnp.dot(hb, w_ref[:, o0:o1], preferred_element_type=F32)
    zc_ref[...] = jnp.dot(hb, w_ref[:, o1:o2], preferred_element_type=F32)
    zba_ref[...] = jnp.dot(hb, w_ref[:, o2:o2 + LANE], preferred_element_type=F32)


def _inproj(x2, lnw, w, layer, tm):
    t = x2.shape[0]
    n = w.shape[2]
    row = lambda i: (i, 0)
    return pl.pallas_call(
        _inproj_kernel,
        grid=(t // tm,),
        in_specs=[pl.BlockSpec((tm, D_MODEL), row),
                  _layer_param_spec((1, D_MODEL), layer),
                  _layer_weight_spec((D_MODEL, n), layer)],
        out_specs=[pl.BlockSpec((tm, RWKV_IN), row),
                   pl.BlockSpec((tm, GDN_MAIN), row),
                   pl.BlockSpec((tm, ATTN_IN), row),
                   pl.BlockSpec((tm, LANE), row)],
        out_shape=[jax.ShapeDtypeStruct((t, RWKV_IN), F32),
                   jax.ShapeDtypeStruct((t, GDN_MAIN), F32),
                   jax.ShapeDtypeStruct((t, ATTN_IN), F32),
                   jax.ShapeDtypeStruct((t, LANE), F32)],
        compiler_params=pltpu.CompilerParams(dimension_semantics=("parallel",),
                                             vmem_limit_bytes=VMEM_LIMIT),
        name="inproj",
    )(x2, lnw, w)


def _rwkv_chunk(q, c, carry, z_ref, mu_ref, w0_ref, wup_ref, a0_ref, aup_ref, gup_ref, kk_ref, ka_ref, rk_ref,
                lnw_ref, lnb_ref, ho_ref, ho32_ref, tri_ref, cum_ref, o_ref, prev_ref, st_ref, tb):
    w = MIX_W
    nchunk = tb // CHUNK
    ext = _shifted_rows(z_ref, prev_ref, q, c)
    z = ext[8:]
    zprev = pltpu.roll(ext, 1, axis=0)[8:]
    zz = z + (zprev - z) * mu_ref[...]
    r = zz[:, 0:w]
    k = zz[:, w:2 * w]
    v = zz[:, 2 * w:3 * w]
    dwa = zz[:, 3 * w:3 * w + LANE]
    dg = zz[:, 3 * w + LANE:]
    yield
    log_w = -_softplus(-(w0_ref[...] + _dot(jnp.tanh(dwa), wup_ref[...]))) - 0.5
    ld = -jnp.exp(log_w)
    iclr =_sigmoid(a0_ref[...] + _dot(dwa, aup_ref[...]))
    gate = _dot(_sigmoid(dg), gup_ref[...])
    yield
    ho = ho_ref[...]
    kkv = k * kk_ref[...]
    k2 = k * (1.0 + (iclr - 1.0) * ka_ref[...])
    sums = _split_dot_r(jnp.concatenate([kkv * kkv, r * k2 * rk_ref[...]], axis=0), ho, 1)
    kkn = kkv * lax.rsqrt(sums[:CHUNK] + L2_EPS)
    bonus = sums[CHUNK:] * v
    b = kkn * iclr
    yield
    cum = _split_dot_l(cum_ref[...], ld, 2)
    e_neg = jnp.exp(-cum)
    at = (-kkn) * jnp.exp(cum - ld)
    atb = at.astype(BF16)
    bt = (b * e_neg).astype(BF16)
    kt = (k2 * e_neg).astype(BF16)
    rt = r * jnp.exp(cum)
    at_rows =_heads_to_rows(at, False)
    v_rows = _heads_to_rows(v, True).astype(BF16)
    yield
    strict, incl = _tri_masks()
    g = lax.dot_general(jnp.concatenate([atb, rt.astype(BF16)], axis=0),
                        jnp.concatenate([_stack_heads(bt, ho), _stack_heads(kt, ho)], axis=0),
                        (((1,), (1,)), ((), ())), preferred_element_type=F32)
    yield
    a_ab = _stack_heads(jnp.where(strict, g[:CHUNK, :w], 0.0).astype(BF16), ho)
    a_ak = _stack_heads(jnp.where(strict, g[:CHUNK, w:], 0.0).astype(BF16), ho)
    a_rb = _stack_heads(jnp.where(incl, g[CHUNK:, :w], 0.0).astype(BF16), ho)
    a_rk = _stack_heads(jnp.where(incl, g[CHUNK:, w:], 0.0).astype(BF16), ho)
    yield
    t_inv = None
    for t_inv in _tri_inv_stages(a_ab, tri_ref):
        if t_inv is None:
            yield
    pv = jnp.dot(jnp.concatenate([a_ak, a_rk], axis=0), v_rows, preferred_element_type=F32)
    yield
    wu = jnp.dot(t_inv, (at_rows + pv[:w]).astype(BF16), preferred_element_type=F32)
    yield
    qy = jnp.dot(a_rb, wu.astype(BF16), preferred_element_type=F32) + pv[w:]
    yield
    w2 = _rows_to_heads(wu, False)
    u0 = _rows_to_heads(wu, True)
    qp = rt + _rows_to_heads(qy, False)
    y0 = _rows_to_heads(qy, True)
    last = cum[CHUNK - 1:CHUNK, :]
    to_end = jnp.exp(last - cum)
    upd_lhs_t = jnp.concatenate([b * to_end, k2 * to_end], axis=0).T.astype(BF16)
    decay_col = jnp.broadcast_to(jnp.exp(last), (8, w)).T[:, 0:1]
    yield
    if c == 0:
        carry["h"] = st_ref[q]
    y = None
    for y in _chain_step(carry, qp, y0, w2, u0, v, upd_lhs_t, decay_col, ho32_ref[...]):
        if y is None:
            yield
    if c == nchunk - 1:
        st_ref[q] = carry["h"]
        prev_ref[q] = z_ref[q, tb - 8:tb, :]
    yield
    inv_n = 1.0 / HEAD_DIM
    mean = _split_dot_r(y, ho, 1) * inv_n
    d = y - mean
    yield
    var = _split_dot_r(d * d, ho, 1) * inv_n
    yn = d * lax.rsqrt(var + RWKV_GN_EPS) * lnw_ref[...] + lnb_ref[...]
    o_ref[q, c * CHUNK:(c + 1) * CHUNK, :] = ((yn + bonus) * gate).astype(o_ref.dtype)


def _rwkv_kernel(*refs, tb, nseq):
    _mixer_body(_rwkv_chunk, refs, tb, nseq)


def _rwkv(za, mu, w0, wup, a0, aup, gup, k_k, k_a, r_k, lnw, lnb, consts, layer, tb, nseq):
    bsz, s, _ = za.shape
    vec = lambda n: _layer_param_spec((1, n), layer)
    mat = lambda m, n: _layer_param_spec((m, n), layer)
    return pl.pallas_call(
        functools.partial(_rwkv_kernel, tb=tb, nseq=nseq),
        grid=(bsz // nseq, s // tb),
        in_specs=[pl.BlockSpec((nseq, tb, RWKV_IN), lambda b, j: (b, j, 0)),
                  vec(RWKV_IN), vec(MIX_W), mat(LANE, MIX_W), vec(MIX_W), mat(LANE, MIX_W),
                  mat(RWKV_GATE_RANK, MIX_W), vec(MIX_W), vec(MIX_W), vec(MIX_W), vec(MIX_W), vec(MIX_W)]
        + _const_specs(),
        out_specs=pl.BlockSpec((nseq, tb, MIX_W), lambda b, j: (b, j, 0)),
        out_shape=jax.ShapeDtypeStruct((bsz, s, MIX_W), BF16),
        scratch_shapes=[pltpu.VMEM((nseq, 8, RWKV_IN), F32), pltpu.VMEM((nseq, MIX_W, MIX_W), F32)],
        compiler_params=pltpu.CompilerParams(dimension_semantics=("parallel", "arbitrary"),
                                             vmem_limit_bytes=VMEM_LIMIT),
        name="rwkv7",
    )(za, mu, w0, wup, a0, aup, gup, k_k, k_a, r_k, lnw, lnb, *consts)


def _gdn_chunk(q, c, carry, z_ref, ba_ref, cw_ref, alog_ref, dtb_ref, nw_ref, ho_ref, ho32_ref, tri_ref, cum_ref,
               o_ref, prev_ref, st_ref, tb):
    w = MIX_W
    nchunk = tb // CHUNK
    ext = _shifted_rows(z_ref, prev_ref, q, c)
    x = ext[:, :3 * w]
    gate = ext[8:, 3 * w:]
    acc = x[8:] * cw_ref[GDN_CONV - 1:GDN_CONV, :]
    for s in range(1, GDN_CONV):
        acc = acc + pltpu.roll(x, s, axis=0)[8:] * cw_ref[GDN_CONV - 1 - s:GDN_CONV - s, :]
    yield
    qkv = acc * _sigmoid(acc)
    qh = qkv[:, :w]
    k = qkv[:, w:2 * w]
    v = qkv[:, 2 * w:]
    ho = ho_ref[...]
    sums = _split_dot_r(jnp.concatenate([qh * qh, k * k], axis=0), ho, 1)
    qn = qh * lax.rsqrt(sums[:CHUNK] + L2_EPS) * (HEAD_DIM ** -0.5)
    kn = k * lax.rsqrt(sums[CHUNK:] + L2_EPS)
    yield
    ba = ba_ref[q, c * CHUNK:(c + 1) * CHUNK, :]
    er = _iota2((LANE, 2 * w), 0)
    ec = _iota2((LANE, 2 * w), 1)
    spread = (er == (ec % w) // HEAD_DIM + N_HEADS * (ec // w)).astype(BF16)
    ba_full = _split_dot_r(ba, spread, 2)
    beta = _sigmoid(ba_full[:, :w])
    g = -jnp.exp(alog_ref[...]) * _softplus(ba_full[:, w:] + dtb_ref[...])
    yield
    gc = _split_dot_l(cum_ref[...], g, 3)
    gt = gc.T
    eg = jnp.exp(gc)
    kb = kn * beta
    qe = qn * eg
    kv_rows = (_heads_to_rows(kb * eg, False) + _heads_to_rows(v * beta, True)).astype(BF16)
    yield
    strict, incl = _tri_masks()
    s1 = lax.dot_general(jnp.concatenate([kb, qn], axis=0).astype(BF16), _stack_heads(kn.astype(BF16), ho),
                         (((1,), (1,)), ((), ())), preferred_element_type=F32)
    yield
    grow = jnp.concatenate([gt[h * HEAD_DIM:h * HEAD_DIM + 1, :] for h in range(N_HEADS)], axis=1)
    dec = jnp.exp(jnp.where(incl, gc - grow, -jnp.inf))
    lower = _stack_heads(jnp.where(strict, -(s1[:CHUNK] * dec), 0.0).astype(BF16), ho)
    intra = _stack_heads((s1[CHUNK:] * dec).astype(BF16), ho)
    yield
    t_inv = None
    for t_inv in _tri_inv_stages(lower, tri_ref):
        if t_inv is None:
            yield
    wu = jnp.dot(t_inv, kv_rows, preferred_element_type=F32)
    yield
    qy = jnp.dot(intra, wu.astype(BF16), preferred_element_type=F32)
    yield
    w2 = -_rows_to_heads(wu, False)
    u0 = _rows_to_heads(wu, True)
    qp = qe - _rows_to_heads(qy, False)
    y0 = _rows_to_heads(qy, True)
    last = gc[CHUNK - 1:CHUNK, :]
    kdec_t = (kn * jnp.exp(last - gc)).T.astype(BF16)
    yield
    if c == 0:
        carry["h"] = st_ref[q]
    o = None
    for o in _chain_step(carry, qp, y0, w2, u0, None, kdec_t, jnp.exp(last), ho32_ref[...]):
        if o is None:
            yield
    if c == nchunk - 1:
        st_ref[q] = carry["h"]
        prev_ref[q] = z_ref[q, tb - 8:tb, :]
    yield
    ms = _split_dot_r(o * o, ho, 1) * (1.0 / HEAD_DIM)
    on = o * lax.rsqrt(ms + RMS_EPS) * nw_ref[...]
    o_ref[q, c * CHUNK:(c + 1) * CHUNK, :] = (on * (gate * _sigmoid(gate))).astype(o_ref.dtype)


def _gdn_kernel(*refs, tb, nseq):
    _mixer_body(_gdn_chunk, refs, tb, nseq)


def _gdn(zb, zba, cw, alog, dtb, nw, consts, layer, tb, nseq):
    bsz, s, _ = zb.shape
    vec = lambda n: _layer_param_spec((1, n), layer)
    return pl.pallas_call(
        functools.partial(_gdn_kernel, tb=tb, nseq=nseq),
        grid=(bsz // nseq, s // tb),
        in_specs=[pl.BlockSpec((nseq, tb, GDN_MAIN), lambda b, j: (b, j, 0)),
                  pl.BlockSpec((nseq, tb, LANE), lambda b, j: (b, j, 0)),
                  _layer_param_spec((GDN_CONV, 3 * MIX_W), layer),
                  vec(MIX_W), vec(MIX_W), vec(MIX_W)] + _const_specs(),
        out_specs=pl.BlockSpec((nseq, tb, MIX_W), lambda b, j: (b, j, 0)),
        out_shape=jax.ShapeDtypeStruct((bsz, s, MIX_W), BF16),
        scratch_shapes=[pltpu.VMEM((nseq, 8, GDN_MAIN), F32), pltpu.VMEM((nseq, MIX_W, MIX_W), F32)],
        compiler_params=pltpu.CompilerParams(dimension_semantics=("parallel", "arbitrary"),
                                             vmem_limit_bytes=VMEM_LIMIT),
        name="gdn",
    )(zb, zba, cw, alog, dtb, nw, *consts)


def _t5_bucket_table():
    L = ATTN_BLOCK
    i = np.arange(L)[None, :]
    j = np.arange(2 * L)[:, None]
    dist = np.maximum(i + L - j, 0)
    max_exact = NUM_BUCKETS // 2
    nf = np.maximum(dist, max_exact).astype(np.float32)
    large = max_exact + (np.log(nf / max_exact) / math.log(MAX_DISTANCE / max_exact)
                         * (NUM_BUCKETS - max_exact)).astype(np.int32)
    large = np.minimum(large, NUM_BUCKETS - 1)
    return np.where(dist < max_exact, dist, large).astype(np.int32)


def _bias_kernel(bkt_ref, rel_ref, o_ref):
    L = ATTN_BLOCK
    grp = ATTN_Q_HEADS // ATTN_KV_HEADS
    bkt = bkt_ref[...]
    kj = _iota2(bkt.shape, 0)
    dist = _iota2(bkt.shape, 1) + L - kj
    in_window = (dist >= 0) & (dist < WINDOW)
    for h in range(ATTN_Q_HEADS):
        acc = jnp.zeros(bkt.shape, F32)
        for b in range(NUM_BUCKETS):
            acc = jnp.where(bkt == b, rel_ref[b, h], acc)
        cols = slice((h % grp) * L, (h % grp + 1) * L)
        o_ref[1, h // grp, :, cols] = jnp.where(in_window, acc, -jnp.inf)
        o_ref[0, h // grp, :, cols] = jnp.where(in_window & (kj >= L), acc, -jnp.inf)


def _bias_band(rel_bias):
    L = ATTN_BLOCK
    shape = (2, ATTN_KV_HEADS, 2 * L, (ATTN_Q_HEADS // ATTN_KV_HEADS) * L)
    return pl.pallas_call(
        _bias_kernel,
        in_specs=[pl.BlockSpec((2 * L, L), lambda: (0, 0)),
                  pl.BlockSpec(memory_space=pltpu.SMEM)],
        out_specs=pl.BlockSpec(shape, lambda: (0, 0, 0, 0)),
        out_shape=jax.ShapeDtypeStruct(shape, F32),
        name="t5_bias",
    )(jnp.asarray(_t5_bucket_table()), rel_bias)


def _swa_kernel(z_ref, bias0_ref, bias_ref, qnw_ref, knw_ref, sink_ref, o_ref, kprev_ref, vprev_ref, *, nsub, layer):
    @pl.when(pl.program_id(1) == 0)
    def _():
        kprev_ref[...] = jnp.zeros_like(kprev_ref)
        vprev_ref[...] = jnp.zeros_like(vprev_ref)

    L = ATTN_BLOCK
    tq = nsub * L
    grp = ATTN_Q_HEADS // ATTN_KV_HEADS
    ngroups = ATTN_W // LANE
    z = z_ref[0]
    k = z[:, ATTN_W:ATTN_W + ATTN_KV_W]
    v = z[:, ATTN_W + ATTN_KV_W:]
    qg = [z[:, p * LANE:(p + 1) * LANE] for p in range(ngroups)]
    sq = _split_dot_r(jnp.concatenate([x * x for x in qg] + [k * k], axis=0), _head_ones(LANE), 2)
    inv_d = 1.0 / HEAD_DIM
    qscale = qnw_ref[...] * (HEAD_DIM ** -0.5)
    qn = [qg[p] * lax.rsqrt(sq[p * tq:(p + 1) * tq] * inv_d + RMS_EPS) * qscale for p in range(ngroups)]
    kn = k * lax.rsqrt(sq[ngroups * tq:] * inv_d + RMS_EPS) * knw_ref[...]
    kall = jnp.concatenate([kprev_ref[...], kn], axis=0)
    vall = jnp.concatenate([vprev_ref[...], v], axis=0)
    kprev_ref[...] = kn[tq - L:]
    vprev_ref[...] = v[tq - L:]
    kroll = pltpu.roll(kall, HEAD_DIM, axis=1)
    vroll = pltpu.roll(vall, HEAD_DIM, axis=1)
    lo_k = _iota2(kall.shape, 1) < HEAD_DIM
    lo_q = _iota2((L, LANE), 1) < HEAD_DIM
    kd = [jnp.where(lo_k, kall, kroll), jnp.where(lo_k, kroll, kall)]
    vdt = [jnp.where(lo_k, vall, vroll).T, jnp.where(lo_k, vroll, vall).T]

    qlane_head = _iota2((1, grp * L), 1) // L

    def block_head(i, j):
        rows = slice(i * L, (i + 1) * L)
        keys = slice(i * L, (i + 2) * L)
        qs = []
        for g in range(grp):
            hq = grp * j + g
            qrow = qn[hq // 2][rows]
            qs.append(jnp.where(lo_q, qrow, 0.0) if hq % 2 == 0 else jnp.where(lo_q, 0.0, qrow))
        bias = bias0_ref[0, j] if i == 0 else bias_ref[0, j]
        sc = _dot_nt(kd[j][keys], jnp.concatenate(qs, axis=0)) + bias
        yield
        sink = jnp.zeros((1, grp * L), F32)
        for g in range(grp):
            sink = jnp.where(qlane_head == g, sink_ref[layer, grp * j + g], sink)
        m = jnp.maximum(jnp.max(sc, axis=0, keepdims=True), sink)
        pe = jnp.exp(sc - m)
        den = jnp.sum(pe, axis=0, keepdims=True) + jnp.exp(sink - m)
        yield
        ot = _dot(vdt[j][:, keys], pe) / den
        yield
        o = ot.T
        for g in range(0, grp, 2):
            o_ref[0, rows, (grp * j + g) // 2 * LANE:((grp * j + g) // 2 + 1) * LANE] = jnp.where(
                lo_q, o[g * L:(g + 1) * L], o[(g + 1) * L:(g + 2) * L]).astype(o_ref.dtype)

    gens = [block_head(i, j) for i in range(nsub) for j in range(ATTN_KV_HEADS)]
    _run_staggered(gens, list(range(len(gens))))


def _swa(zc, bias, qnw, knw, sinks, layer, tq):
    bsz, s, _ = zc.shape
    L = ATTN_BLOCK
    bias_block = (1,) + bias.shape[1:]
    return pl.pallas_call(
        functools.partial(_swa_kernel, nsub=tq // L, layer=layer),
        grid=(bsz, s // tq),
        in_specs=[pl.BlockSpec((1, tq, ATTN_IN), lambda b, n: (b, n, 0)),
                  pl.BlockSpec(bias_block, lambda b, n: (jnp.minimum(n, 1), 0, 0, 0)),
                  pl.BlockSpec(bias_block, lambda b, n: (1, 0, 0, 0)),
                  _layer_param_spec((1, LANE), layer),
                  _layer_param_spec((1, LANE), layer),
                  pl.BlockSpec(memory_space=pltpu.SMEM)],
        out_specs=pl.BlockSpec((1, tq, ATTN_W), lambda b, n: (b, n, 0)),
        out_shape=jax.ShapeDtypeStruct((bsz, s, ATTN_W), BF16),
        scratch_shapes=[pltpu.VMEM((L, ATTN_KV_W), F32), pltpu.VMEM((L, ATTN_KV_W), F32)],
        compiler_params=pltpu.CompilerParams(dimension_semantics=("parallel", "arbitrary"),
                                             vmem_limit_bytes=VMEM_LIMIT),
        name="swa",
    )(zc, bias, bias, qnw, knw, sinks)


def _mlp_kernel(x_ref, ya_ref, yb_ref, yc_ref, wo_ref, ln2_ref, w1_ref, w2_ref, o_ref, *, tf):
    y = (jnp.dot(ya_ref[...], wo_ref[0:MIX_W, :], preferred_element_type=F32)
         + jnp.dot(yb_ref[...], wo_ref[MIX_W:2 * MIX_W, :], preferred_element_type=F32)
         + jnp.dot(yc_ref[...], wo_ref[2 * MIX_W:, :], preferred_element_type=F32))
    x1 = x_ref[...] + y
    h = (x1 * lax.rsqrt(jnp.mean(x1 * x1, axis=-1, keepdims=True) + RMS_EPS) * ln2_ref[...]).astype(BF16)
    acc = x1
    for j in range(D_FF // tf):
        u = jnp.dot(h, w1_ref[:, j * tf:(j + 1) * tf], preferred_element_type=F32)
        u = jnp.square(jnp.maximum(u, 0.0))
        acc = acc + jnp.dot(u.astype(BF16), w2_ref[j * tf:(j + 1) * tf, :], preferred_element_type=F32)
    o_ref[...] = acc


def _layer_weight_spec(shape, layer):
    return pl.BlockSpec((None,) + shape, lambda *_: (layer, 0, 0), pipeline_mode=pl.Buffered(1))


def _layer_param_spec(shape, layer):
    return pl.BlockSpec((None,) + shape, lambda *_: (layer, 0, 0))


def _out_mlp(x2, ya, yb, yc, wo, ln2, w1, w2, layer, tm, tf):
    t = x2.shape[0]
    row = lambda i: (i, 0)
    return pl.pallas_call(
        functools.partial(_mlp_kernel, tf=tf),
        grid=(t // tm,),
        in_specs=[pl.BlockSpec((tm, D_MODEL), row),
                  pl.BlockSpec((tm, MIX_W), row),
                  pl.BlockSpec((tm, MIX_W), row),
                  pl.BlockSpec((tm, ATTN_W), row),
                  _layer_weight_spec((D_MODEL, D_MODEL), layer),
                  _layer_param_spec((1, D_MODEL), layer),
                  _layer_weight_spec((D_MODEL, D_FF), layer),
                  _layer_weight_spec((D_FF, D_MODEL), layer)],
        out_specs=pl.BlockSpec((tm, D_MODEL), row),
        out_shape=jax.ShapeDtypeStruct((t, D_MODEL), F32),
        compiler_params=pltpu.CompilerParams(dimension_semantics=("parallel",),
                                             vmem_limit_bytes=VMEM_LIMIT),
        name="out_mlp",
    )(x2, ya, yb, yc, wo, ln2, w1, w2)


def _pad_rows(m, rows, offset):
    return jnp.zeros((m.shape[0], rows, m.shape[2]), m.dtype).at[:, offset:offset + m.shape[1]].set(m)


def _prepare_params(p):
    depth = p["w_in"].shape[0]
    vec = lambda a: a.reshape(depth, 1, -1)
    per_head = lambda a: jnp.repeat(a, HEAD_DIM, axis=1)[:, None, :]
    tiled = lambda a, n: jnp.tile(a, (1, n))[:, None, :]
    w_in = p["w_in"]
    o_ba = RWKV_IN + GDN_MAIN
    o_c = RWKV_IN + GDN_IN
    w_re = jnp.concatenate(
        [w_in[:, :, :o_ba], w_in[:, :, o_c:], w_in[:, :, o_ba:o_c],
         jnp.zeros((depth, D_MODEL, LANE - 2 * N_HEADS), w_in.dtype)], axis=2)
    return dict(
        w_in=w_re.astype(BF16), w_out=p["w_out"].astype(BF16),
        w_ff1=p["w_ff1"].astype(BF16), w_ff2=p["w_ff2"].astype(BF16),
        ln1_w=vec(p["ln1_w"]), ln2_w=vec(p["ln2_w"]),
        rwkv_mu=vec(p["rwkv_mu"]), rwkv_w0=vec(p["rwkv_w0"]), rwkv_a0=vec(p["rwkv_a0"]),
        rwkv_w_up=_pad_rows(p["rwkv_w_up"], LANE, 0).astype(BF16),
        rwkv_a_up=_pad_rows(p["rwkv_a_up"], LANE, RWKV_DECAY_RANK).astype(BF16),
        rwkv_g_up=p["rwkv_g_up"].astype(BF16),
        rwkv_k_k=vec(p["rwkv_k_k"]), rwkv_k_a=vec(p["rwkv_k_a"]), rwkv_r_k=vec(p["rwkv_r_k"]),
        rwkv_lnx_w=vec(p["rwkv_lnx_w"]), rwkv_lnx_b=vec(p["rwkv_lnx_b"]),
        gdn_conv_w=p["gdn_conv_w"], gdn_a_log=per_head(p["gdn_a_log"]), gdn_dt_bias=per_head(p["gdn_dt_bias"]),
        gdn_norm_w=tiled(p["gdn_norm_w"], N_HEADS),
        attn_q_norm_w=tiled(p["attn_q_norm_w"], LANE // HEAD_DIM),
        attn_k_norm_w=tiled(p["attn_k_norm_w"], LANE // HEAD_DIM),
        attn_sinks=p["attn_sinks"])


def _layer(x2, bsz, s, bias, consts, p, layer):
    t = bsz * s
    tb = min(MIX_TB, s)
    nseq = MIX_NSEQ if bsz % MIX_NSEQ == 0 else 1
    za, zb, zc, zba = _inproj(x2, p["ln1_w"], p["w_in"], layer, min(1024, t))
    ya = _rwkv(za.reshape(bsz, s, RWKV_IN), p["rwkv_mu"], p["rwkv_w0"], p["rwkv_w_up"], p["rwkv_a0"],
               p["rwkv_a_up"], p["rwkv_g_up"], p["rwkv_k_k"], p["rwkv_k_a"], p["rwkv_r_k"],
               p["rwkv_lnx_w"], p["rwkv_lnx_b"], consts, layer, tb, nseq)
    yb = _gdn(zb.reshape(bsz, s, GDN_MAIN), zba.reshape(bsz, s, LANE), p["gdn_conv_w"], p["gdn_a_log"],
              p["gdn_dt_bias"], p["gdn_norm_w"], consts, layer, tb, nseq)
    yc = _swa(zc.reshape(bsz, s, ATTN_IN), bias, p["attn_q_norm_w"], p["attn_k_norm_w"], p["attn_sinks"],
              layer, min(SWA_TQ, s))
    return _out_mlp(x2, ya.reshape(t, MIX_W), yb.reshape(t, MIX_W), yc.reshape(t, ATTN_W),
                    p["w_out"], p["ln2_w"], p["w_ff1"], p["w_ff2"], layer, min(1024, t), min(1024, D_FF))


def kernel(x, ln1_w, w_in, rwkv_mu, rwkv_w0, rwkv_w_up, rwkv_a0, rwkv_a_up, rwkv_g_up, rwkv_k_k, rwkv_k_a,
           rwkv_r_k, rwkv_lnx_w, rwkv_lnx_b, gdn_conv_w, gdn_a_log, gdn_dt_bias, gdn_norm_w, attn_q_norm_w,
           attn_k_norm_w, attn_sinks, rel_bias, w_out, ln2_w, w_ff1, w_ff2):
    stacked = dict(ln1_w=ln1_w, w_in=w_in, rwkv_mu=rwkv_mu, rwkv_w0=rwkv_w0, rwkv_w_up=rwkv_w_up,
                   rwkv_a0=rwkv_a0, rwkv_a_up=rwkv_a_up, rwkv_g_up=rwkv_g_up, rwkv_k_k=rwkv_k_k,
                   rwkv_k_a=rwkv_k_a, rwkv_r_k=rwkv_r_k, rwkv_lnx_w=rwkv_lnx_w, rwkv_lnx_b=rwkv_lnx_b,
                   gdn_conv_w=gdn_conv_w, gdn_a_log=gdn_a_log, gdn_dt_bias=gdn_dt_bias, gdn_norm_w=gdn_norm_w,
                   attn_q_norm_w=attn_q_norm_w, attn_k_norm_w=attn_k_norm_w, attn_sinks=attn_sinks,
                   w_out=w_out, ln2_w=ln2_w, w_ff1=w_ff1, w_ff2=w_ff2)
    bsz, s, _ = x.shape
    bias = _bias_band(rel_bias)
    consts = _mixer_consts()
    params = _prepare_params(stacked)
    x2 = x.reshape(bsz * s, D_MODEL)
    for layer in range(w_in.shape[0]):
        x2 = _layer(x2, bsz, s, bias, consts, params, layer)
    return x2.reshape(bsz, s, D_MODEL)
```

```python
import collections
import functools
import math

import numpy as np
import jax
import jax.numpy as jnp
from jax import lax
from jax.experimental import pallas as pl
from jax.experimental.pallas import tpu as pltpu

F32 = jnp.float32
BF16 = jnp.bfloat16

D_MODEL = 1024
DEPTH = 4
HEAD_DIM = 64
RMS_EPS = 1e-6
L2_EPS = 1e-6
N_HEADS = 4
MIX_W = N_HEADS * HEAD_DIM
RWKV_DECAY_RANK = 64
RWKV_ICLR_RANK = 64
RWKV_GATE_RANK = 128
RWKV_GN_EPS = 64e-5
RWKV_IN = 3 * MIX_W + RWKV_DECAY_RANK + RWKV_ICLR_RANK + RWKV_GATE_RANK
GDN_CONV = 4
GDN_MAIN = 4 * MIX_W
GDN_IN = GDN_MAIN + 2 * N_HEADS
ATTN_Q_HEADS = 8
ATTN_KV_HEADS = 2
ATTN_W = ATTN_Q_HEADS * HEAD_DIM
ATTN_KV_W = ATTN_KV_HEADS * HEAD_DIM
ATTN_IN = ATTN_W + 2 * ATTN_KV_W
ATTN_BLOCK = 128
WINDOW = 128
NUM_BUCKETS = 32
MAX_DISTANCE = 128
D_FF = 4 * D_MODEL
LANE = 128
F32_ROWS = 8
CHUNK = 64
MIX_TB = 256
MIX_NSEQ = 2
MIX_CHUNK_LEAD = 2
MIX_SEQ_LEAD = 1
SWA_TQ = 2048
VMEM_LIMIT = 56 * 1024 * 1024


def _dot(a, b):
    return jnp.dot(a.astype(BF16), b.astype(BF16), preferred_element_type=F32)


def _dot_nt(a, b):
    return lax.dot_general(a.astype(BF16), b.astype(BF16), (((1,), (1,)), ((), ())),
                           preferred_element_type=F32)


def _bf16_parts(x, parts):
    out = []
    r = x
    for p in range(parts):
        hi = r.astype(BF16)
        out.append(hi)
        if p + 1 < parts:
            r = r - hi.astype(F32)
    return out


def _split_dot_r(x, m_bf16, parts):
    rows = x.shape[0]
    t = jnp.dot(jnp.concatenate(_bf16_parts(x, parts), axis=0), m_bf16, preferred_element_type=F32)
    acc = t[:rows]
    for p in range(1, parts):
        acc = acc + t[p * rows:(p + 1) * rows]
    return acc


def _split_dot_l(m_bf16, x, parts):
    cols = x.shape[1]
    t = jnp.dot(m_bf16, jnp.concatenate(_bf16_parts(x, parts), axis=1), preferred_element_type=F32)
    acc = t[:, :cols]
    for p in range(1, parts):
        acc = acc + t[:, p * cols:(p + 1) * cols]
    return acc


def _sigmoid(x):
    return 1.0 / (1.0 + jnp.exp(-x))


def _softplus(x):
    return jnp.maximum(x, 0.0) + jnp.log(1.0 + jnp.exp(-jnp.abs(x)))


def _iota2(shape, dim):
    return lax.broadcasted_iota(jnp.int32, shape, dim)


def _head_ones(n):
    r = _iota2((n, n), 0) // HEAD_DIM
    c = _iota2((n, n), 1) // HEAD_DIM
    return (r == c).astype(BF16)


def _tri_masks():
    t = _iota2((CHUNK, MIX_W), 0)
    s = _iota2((CHUNK, MIX_W), 1) % HEAD_DIM
    return s < t, s <= t


def _run_staggered(gens, starts):
    done = [False] * len(gens)
    rnd = 0
    while not all(done):
        for i, g in enumerate(gens):
            if rnd >= starts[i] and not done[i]:
                try:
                    next(g)
                except StopIteration:
                    done[i] = True
        rnd += 1


MixConsts = collections.namedtuple("MixConsts", "ho ho32 tri cum")


def _mixer_consts():
    n = MIX_W
    r = np.arange(n)[:, None]
    c = np.arange(n)[None, :]
    ho = (r // HEAD_DIM) == (c // HEAD_DIM)
    tri = [r == c, (r // 2) == (c // 2)]
    s = 2
    while s < CHUNK:
        tri.append(((r // (2 * s)) == (c // (2 * s))) & ((r // s) % 2 == 1) & ((c // s) % 2 == 0))
        s *= 2
    cum = np.arange(CHUNK)[None, :] <= np.arange(CHUNK)[:, None]
    return MixConsts(jnp.asarray(ho, BF16), jnp.asarray(ho, F32), jnp.asarray(np.stack(tri), BF16),
                     jnp.asarray(cum, BF16))


def _const_specs():
    fixed = lambda shape: pl.BlockSpec(shape, lambda b, j: (0,) * len(shape), pipeline_mode=pl.Buffered(1))
    ntri = int(math.log2(CHUNK)) + 1
    return [fixed((MIX_W, MIX_W)), fixed((MIX_W, MIX_W)), fixed((ntri, MIX_W, MIX_W)), fixed((CHUNK, CHUNK))]


def _stack_heads(xb, ho):
    return jnp.concatenate([xb] * N_HEADS, axis=0) * ho


def _heads_to_rows(x, hi_half):
    lo = _iota2((x.shape[0], LANE), 1) < HEAD_DIM
    blocks = []
    for h in range(N_HEADS):
        col = x[:, (h // 2) * LANE:(h // 2 + 1) * LANE]
        if (h % 2 == 1) != hi_half:
            col = pltpu.roll(col, HEAD_DIM, axis=1)
        blocks.append(jnp.where(lo != hi_half, col, 0.0))
    return jnp.concatenate(blocks, axis=0)


def _rows_to_heads(z, hi_half):
    c = z.shape[0] // N_HEADS
    lo = _iota2((c, LANE), 1) < HEAD_DIM
    cols = []
    for p in range(N_HEADS // 2):
        even = z[2 * p * c:(2 * p + 1) * c]
        odd = z[(2 * p + 1) * c:(2 * p + 2) * c]
        if hi_half:
            even = pltpu.roll(even, HEAD_DIM, axis=1)
        else:
            odd = pltpu.roll(odd, HEAD_DIM, axis=1)
        cols.append(jnp.where(lo, even, odd))
    return jnp.concatenate(cols, axis=1)


def _tri_inv_stages(a, tri_ref):
    n = a.shape[0]
    t = tri_ref[0] + a * tri_ref[1]
    s = 2
    for lvl in range(2, tri_ref.shape[0]):
        a_off = a * tri_ref[lvl]
        if s % F32_ROWS:
            inner = jnp.dot(a_off, t, preferred_element_type=F32).astype(BF16)
            yield None
            t = t + jnp.dot(t, inner, preferred_element_type=F32).astype(BF16)
        else:
            odd = [slice(r0 + s, r0 + 2 * s) for r0 in range(0, n, 2 * s)]
            zero = jnp.zeros((s, n), BF16)
            pick = lambda x: jnp.concatenate([x[r] for r in odd], axis=0)
            spread = lambda x: jnp.concatenate(
                [blk for i in range(len(odd)) for blk in (zero, x[i * s:(i + 1) * s])], axis=0)
            inner = spread(jnp.dot(pick(a_off), t, preferred_element_type=F32).astype(BF16))
            yield None
            t = t + spread(jnp.dot(pick(t), inner, preferred_element_type=F32).astype(BF16))
        yield None
        s *= 2
    yield t


def _chain_step(carry, qp, y0, w2, u0, upd_extra, upd_lhs_t, decay, ho32):
    h = carry["h"]
    uy = _dot(jnp.concatenate([w2, qp], axis=0), h)
    yield None
    u = u0 + uy[:CHUNK]
    rows = u if upd_extra is None else jnp.concatenate([u, upd_extra], axis=0)
    carry["h"] = h * decay + jnp.dot(upd_lhs_t, rows.astype(BF16), preferred_element_type=F32) * ho32
    yield uy[CHUNK:] + y0


def _shifted_rows(z_ref, prev_ref, q, c):
    if c == 0:
        return jnp.concatenate([prev_ref[q], z_ref[q, 0:CHUNK, :]], axis=0)
    return z_ref[q, c * CHUNK - 8:(c + 1) * CHUNK, :]


def _mixer_body(chunk_fn, refs, tb, nseq):
    prev_ref, st_ref = refs[-2:]

    @pl.when(pl.program_id(1) == 0)
    def _():
        prev_ref[...] = jnp.zeros_like(prev_ref)
        st_ref[...] = jnp.zeros_like(st_ref)

    assert MIX_CHUNK_LEAD >= 1
    carries = [{} for _ in range(nseq)]
    order = [(c, q) for c in range(tb // CHUNK) for q in range(nseq)]
    gens = [chunk_fn(q, c, carries[q], *refs, tb) for c, q in order]
    _run_staggered(gens, [c * MIX_CHUNK_LEAD + q * MIX_SEQ_LEAD for c, q in order])


def _inproj_kernel(x_ref, lnw_ref, w_ref, za_ref, zb_ref, zc_ref, zba_ref):
    x = x_ref[...]
    h = x * lax.rsqrt(jnp.mean(x * x, axis=-1, keepdims=True) + RMS_EPS) * lnw_ref[...]
    hb = h.astype(BF16)
    o0 = RWKV_IN
    o1 = o0 + GDN_MAIN
    o2 = o1 + ATTN_IN
    za_ref[...] = jnp.dot(hb, w_ref[:, 0:o0], preferred_element_type=F32)
    zb_ref[...] = jnp.dot(hb, w_ref[:, o0:o1], preferred_element_type=F32)
    zc_ref[...] = jnp.dot(hb, w_ref[:, o1:o2], preferred_element_type=F32)
    zba_ref[...] = jnp.dot(hb, w_ref[:, o2:o2 + LANE], preferred_element_type=F32)


def _inproj(x2, lnw, w, layer, tm):
    t = x2.shape[0]
    n = w.shape[2]
    row = lambda i: (i, 0)
    return pl.pallas_call(
        _inproj_kernel,
        grid=(t // tm,),
        in_specs=[pl.BlockSpec((tm, D_MODEL), row),
                  _layer_param_spec((1, D_MODEL), layer),
                  _layer_weight_spec((D_MODEL, n), layer)],
        out_specs=[pl.BlockSpec((tm, RWKV_IN), row),
                   pl.BlockSpec((tm, GDN_MAIN), row),
                   pl.BlockSpec((tm, ATTN_IN), row),
                   pl.BlockSpec((tm, LANE), row)],
        out_shape=[jax.ShapeDtypeStruct((t, RWKV_IN), F32),
                   jax.ShapeDtypeStruct((t, GDN_MAIN), F32),
                   jax.ShapeDtypeStruct((t, ATTN_IN), F32),
                   jax.ShapeDtypeStruct((t, LANE), F32)],
        compiler_params=pltpu.CompilerParams(dimension_semantics=("parallel",),
                                             vmem_limit_bytes=VMEM_LIMIT),
        name="inproj",
    )(x2, lnw, w)


def _rwkv_chunk(q, c, carry, z_ref, mu_ref, w0_ref, wup_ref, a0_ref, aup_ref, gup_ref, kk_ref, ka_ref, rk_ref,
                lnw_ref, lnb_ref, ho_ref, ho32_ref, tri_ref, cum_ref, o_ref, prev_ref, st_ref, tb):
    w = MIX_W
    nchunk = tb // CHUNK
    ext = _shifted_rows(z_ref, prev_ref, q, c)
    z = ext[8:]
    zprev = pltpu.roll(ext, 1, axis=0)[8:]
    zz = z + (zprev - z) * mu_ref[...]
    r = zz[:, 0:w]
    k = zz[:, w:2 * w]
    v = zz[:, 2 * w:3 * w]
    dwa = zz[:, 3 * w:3 * w + LANE]
    dg = zz[:, 3 * w + LANE:]
    yield
    log_w = -_softplus(-(w0_ref[...] + _dot(jnp.tanh(dwa), wup_ref[...]))) - 0.5
    ld = -jnp.exp(log_w)
    iclr =_sigmoid(a0_ref[...] + _dot(dwa, aup_ref[...]))
    gate = _dot(_sigmoid(dg), gup_ref[...])
    yield
    ho = ho_ref[...]
    kkv = k * kk_ref[...]
    k2 = k * (1.0 + (iclr - 1.0) * ka_ref[...])
    sums = _split_dot_r(jnp.concatenate([kkv * kkv, r * k2 * rk_ref[...]], axis=0), ho, 1)
    kkn = kkv * lax.rsqrt(sums[:CHUNK] + L2_EPS)
    bonus = sums[CHUNK:] * v
    b = kkn * iclr
    yield
    cum = _split_dot_l(cum_ref[...], ld, 2)
    e_neg = jnp.exp(-cum)
    at = (-kkn) * jnp.exp(cum - ld)
    atb = at.astype(BF16)
    bt = (b * e_neg).astype(BF16)
    kt = (k2 * e_neg).astype(BF16)
    rt = r * jnp.exp(cum)
    at_rows =_heads_to_rows(at, False)
    v_rows = _heads_to_rows(v, True).astype(BF16)
    yield
    strict, incl = _tri_masks()
    g = lax.dot_general(jnp.concatenate([atb, rt.astype(BF16)], axis=0),
                        jnp.concatenate([_stack_heads(bt, ho), _stack_heads(kt, ho)], axis=0),
                        (((1,), (1,)), ((), ())), preferred_element_type=F32)
    yield
    a_ab = _stack_heads(jnp.where(strict, g[:CHUNK, :w], 0.0).astype(BF16), ho)
    a_ak = _stack_heads(jnp.where(strict, g[:CHUNK, w:], 0.0).astype(BF16), ho)
    a_rb = _stack_heads(jnp.where(incl, g[CHUNK:, :w], 0.0).astype(BF16), ho)
    a_rk = _stack_heads(jnp.where(incl, g[CHUNK:, w:], 0.0).astype(BF16), ho)
    yield
    t_inv = None
    for t_inv in _tri_inv_stages(a_ab, tri_ref):
        if t_inv is None:
            yield
    pv = jnp.dot(jnp.concatenate([a_ak, a_rk], axis=0), v_rows, preferred_element_type=F32)
    yield
    wu = jnp.dot(t_inv, (at_rows + pv[:w]).astype(BF16), preferred_element_type=F32)
    yield
    qy = jnp.dot(a_rb, wu.astype(BF16), preferred_element_type=F32) + pv[w:]
    yield
    w2 = _rows_to_heads(wu, False)
    u0 = _rows_to_heads(wu, True)
    qp = rt + _rows_to_heads(qy, False)
    y0 = _rows_to_heads(qy, True)
    last = cum[CHUNK - 1:CHUNK, :]
    to_end = jnp.exp(last - cum)
    upd_lhs_t = jnp.concatenate([b * to_end, k2 * to_end], axis=0).T.astype(BF16)
    decay_col = jnp.broadcast_to(jnp.exp(last), (8, w)).T[:, 0:1]
    yield
    if c == 0:
        carry["h"] = st_ref[q]
    y = None
    for y in _chain_step(carry, qp, y0, w2, u0, v, upd_lhs_t, decay_col, ho32_ref[...]):
        if y is None:
            yield
    if c == nchunk - 1:
        st_ref[q] = carry["h"]
        prev_ref[q] = z_ref[q, tb - 8:tb, :]
    yield
    inv_n = 1.0 / HEAD_DIM
    mean = _split_dot_r(y, ho, 1) * inv_n
    d = y - mean
    yield
    var = _split_dot_r(d * d, ho, 1) * inv_n
    yn = d * lax.rsqrt(var + RWKV_GN_EPS) * lnw_ref[...] + lnb_ref[...]
    o_ref[q, c * CHUNK:(c + 1) * CHUNK, :] = ((yn + bonus) * gate).astype(o_ref.dtype)


def _rwkv_kernel(*refs, tb, nseq):
    _mixer_body(_rwkv_chunk, refs, tb, nseq)


def _rwkv(za, mu, w0, wup, a0, aup, gup, k_k, k_a, r_k, lnw, lnb, consts, layer, tb, nseq):
    bsz, s, _ = za.shape
    vec = lambda n: _layer_param_spec((1, n), layer)
    mat = lambda m, n: _layer_param_spec((m, n), layer)
    return pl.pallas_call(
        functools.partial(_rwkv_kernel, tb=tb, nseq=nseq),
        grid=(bsz // nseq, s // tb),
        in_specs=[pl.BlockSpec((nseq, tb, RWKV_IN), lambda b, j: (b, j, 0)),
                  vec(RWKV_IN), vec(MIX_W), mat(LANE, MIX_W), vec(MIX_W), mat(LANE, MIX_W),
                  mat(RWKV_GATE_RANK, MIX_W), vec(MIX_W), vec(MIX_W), vec(MIX_W), vec(MIX_W), vec(MIX_W)]
        + _const_specs(),
        out_specs=pl.BlockSpec((nseq, tb, MIX_W), lambda b, j: (b, j, 0)),
        out_shape=jax.ShapeDtypeStruct((bsz, s, MIX_W), BF16),
        scratch_shapes=[pltpu.VMEM((nseq, 8, RWKV_IN), F32), pltpu.VMEM((nseq, MIX_W, MIX_W), F32)],
        compiler_params=pltpu.CompilerParams(dimension_semantics=("parallel", "arbitrary"),
                                             vmem_limit_bytes=VMEM_LIMIT),
        name="rwkv7",
    )(za, mu, w0, wup, a0, aup, gup, k_k, k_a, r_k, lnw, lnb, *consts)


def _gdn_chunk(q, c, carry, z_ref, ba_ref, cw_ref, alog_ref, dtb_ref, nw_ref, ho_ref, ho32_ref, tri_ref, cum_ref,
               o_ref, prev_ref, st_ref, tb):
    w = MIX_W
    nchunk = tb // CHUNK
    ext = _shifted_rows(z_ref, prev_ref, q, c)
    x = ext[:, :3 * w]
    gate = ext[8:, 3 * w:]
    acc = x[8:] * cw_ref[GDN_CONV - 1:GDN_CONV, :]
    for s in range(1, GDN_CONV):
        acc = acc + pltpu.roll(x, s, axis=0)[8:] * cw_ref[GDN_CONV - 1 - s:GDN_CONV - s, :]
    yield
    qkv = acc * _sigmoid(acc)
    qh = qkv[:, :w]
    k = qkv[:, w:2 * w]
    v = qkv[:, 2 * w:]
    ho = ho_ref[...]
    sums = _split_dot_r(jnp.concatenate([qh * qh, k * k], axis=0), ho, 1)
    qn = qh * lax.rsqrt(sums[:CHUNK] + L2_EPS) * (HEAD_DIM ** -0.5)
    kn = k * lax.rsqrt(sums[CHUNK:] + L2_EPS)
    yield
    ba = ba_ref[q, c * CHUNK:(c + 1) * CHUNK, :]
    er = _iota2((LANE, 2 * w), 0)
    ec = _iota2((LANE, 2 * w), 1)
    spread = (er == (ec % w) // HEAD_DIM + N_HEADS * (ec // w)).astype(BF16)
    ba_full = _split_dot_r(ba, spread, 2)
    beta = _sigmoid(ba_full[:, :w])
    g = -jnp.exp(alog_ref[...]) * _softplus(ba_full[:, w:] + dtb_ref[...])
    yield
    gc = _split_dot_l(cum_ref[...], g, 3)
    gt = gc.T
    eg = jnp.exp(gc)
    kb = kn * beta
    qe = qn * eg
    kv_rows = (_heads_to_rows(kb * eg, False) + _heads_to_rows(v * beta, True)).astype(BF16)
    yield
    strict, incl = _tri_masks()
    s1 = lax.dot_general(jnp.concatenate([kb, qn], axis=0).astype(BF16), _stack_heads(kn.astype(BF16), ho),
                         (((1,), (1,)), ((), ())), preferred_element_type=F32)
    yield
    grow = jnp.concatenate([gt[h * HEAD_DIM:h * HEAD_DIM + 1, :] for h in range(N_HEADS)], axis=1)
    dec = jnp.exp(jnp.where(incl, gc - grow, -jnp.inf))
    lower = _stack_heads(jnp.where(strict, -(s1[:CHUNK] * dec), 0.0).astype(BF16), ho)
    intra = _stack_heads((s1[CHUNK:] * dec).astype(BF16), ho)
    yield
    t_inv = None
    for t_inv in _tri_inv_stages(lower, tri_ref):
        if t_inv is None:
            yield
    wu = jnp.dot(t_inv, kv_rows, preferred_element_type=F32)
    yield
    qy = jnp.dot(intra, wu.astype(BF16), preferred_element_type=F32)
    yield
    w2 = -_rows_to_heads(wu, False)
    u0 = _rows_to_heads(wu, True)
    qp = qe - _rows_to_heads(qy, False)
    y0 = _rows_to_heads(qy, True)
    last = gc[CHUNK - 1:CHUNK, :]
    kdec_t = (kn * jnp.exp(last - gc)).T.astype(BF16)
    yield
    if c == 0:
        carry["h"] = st_ref[q]
    o = None
    for o in _chain_step(carry, qp, y0, w2, u0, None, kdec_t, jnp.exp(last), ho32_ref[...]):
        if o is None:
            yield
    if c == nchunk - 1:
        st_ref[q] = carry["h"]
        prev_ref[q] = z_ref[q, tb - 8:tb, :]
    yield
    ms = _split_dot_r(o * o, ho, 1) * (1.0 / HEAD_DIM)
    on = o * lax.rsqrt(ms + RMS_EPS) * nw_ref[...]
    o_ref[q, c * CHUNK:(c + 1) * CHUNK, :] = (on * (gate * _sigmoid(gate))).astype(o_ref.dtype)


def _gdn_kernel(*refs, tb, nseq):
    _mixer_body(_gdn_chunk, refs, tb, nseq)


def _gdn(zb, zba, cw, alog, dtb, nw, consts, layer, tb, nseq):
    bsz, s, _ = zb.shape
    vec = lambda n: _layer_param_spec((1, n), layer)
    return pl.pallas_call(
        functools.partial(_gdn_kernel, tb=tb, nseq=nseq),
        grid=(bsz // nseq, s // tb),
        in_specs=[pl.BlockSpec((nseq, tb, GDN_MAIN), lambda b, j: (b, j, 0)),
                  pl.BlockSpec((nseq, tb, LANE), lambda b, j: (b, j, 0)),
                  _layer_param_spec((GDN_CONV, 3 * MIX_W), layer),
                  vec(MIX_W), vec(MIX_W), vec(MIX_W)] + _const_specs(),
        out_specs=pl.BlockSpec((nseq, tb, MIX_W), lambda b, j: (b, j, 0)),
        out_shape=jax.ShapeDtypeStruct((bsz, s, MIX_W), BF16),
        scratch_shapes=[pltpu.VMEM((nseq, 8, GDN_MAIN), F32), pltpu.VMEM((nseq, MIX_W, MIX_W), F32)],
        compiler_params=pltpu.CompilerParams(dimension_semantics=("parallel", "arbitrary"),
                                             vmem_limit_bytes=VMEM_LIMIT),
        name="gdn",
    )(zb, zba, cw, alog, dtb, nw, *consts)


def _t5_bucket_table():
    L = ATTN_BLOCK
    i = np.arange(L)[None, :]
    j = np.arange(2 * L)[:, None]
    dist = np.maximum(i + L - j, 0)
    max_exact = NUM_BUCKETS // 2
    nf = np.maximum(dist, max_exact).astype(np.float32)
    large = max_exact + (np.log(nf / max_exact) / math.log(MAX_DISTANCE / max_exact)
                         * (NUM_BUCKETS - max_exact)).astype(np.int32)
    large = np.minimum(large, NUM_BUCKETS - 1)
    return np.where(dist < max_exact, dist, large).astype(np.int32)


def _bias_kernel(bkt_ref, rel_ref, o_ref):
    L = ATTN_BLOCK
    grp = ATTN_Q_HEADS // ATTN_KV_HEADS
    bkt = bkt_ref[...]
    kj = _iota2(bkt.shape, 0)
    dist = _iota2(bkt.shape, 1) + L - kj
    in_window = (dist >= 0) & (dist < WINDOW)
    for h in range(ATTN_Q_HEADS):
        acc = jnp.zeros(bkt.shape, F32)
        for b in range(NUM_BUCKETS):
            acc = jnp.where(bkt == b, rel_ref[b, h], acc)
        cols = slice((h % grp) * L, (h % grp + 1) * L)
        o_ref[1, h // grp, :, cols] = jnp.where(in_window, acc, -jnp.inf)
        o_ref[0, h // grp, :, cols] = jnp.where(in_window & (kj >= L), acc, -jnp.inf)


def _bias_band(rel_bias):
    L = ATTN_BLOCK
    shape = (2, ATTN_KV_HEADS, 2 * L, (ATTN_Q_HEADS // ATTN_KV_HEADS) * L)
    return pl.pallas_call(
        _bias_kernel,
        in_specs=[pl.BlockSpec((2 * L, L), lambda: (0, 0)),
                  pl.BlockSpec(memory_space=pltpu.SMEM)],
        out_specs=pl.BlockSpec(shape, lambda: (0, 0, 0, 0)),
        out_shape=jax.ShapeDtypeStruct(shape, F32),
        name="t5_bias",
    )(jnp.asarray(_t5_bucket_table()), rel_bias)


def _swa_kernel(z_ref, bias0_ref, bias_ref, qnw_ref, knw_ref, sink_ref, o_ref, kprev_ref, vprev_ref, *, nsub, layer):
    @pl.when(pl.program_id(1) == 0)
    def _():
        kprev_ref[...] = jnp.zeros_like(kprev_ref)
        vprev_ref[...] = jnp.zeros_like(vprev_ref)

    L = ATTN_BLOCK
    tq = nsub * L
    grp = ATTN_Q_HEADS // ATTN_KV_HEADS
    ngroups = ATTN_W // LANE
    z = z_ref[0]
    k = z[:, ATTN_W:ATTN_W + ATTN_KV_W]
    v = z[:, ATTN_W + ATTN_KV_W:]
    qg = [z[:, p * LANE:(p + 1) * LANE] for p in range(ngroups)]
    sq = _split_dot_r(jnp.concatenate([x * x for x in qg] + [k * k], axis=0), _head_ones(LANE), 2)
    inv_d = 1.0 / HEAD_DIM
    qscale = qnw_ref[...] * (HEAD_DIM ** -0.5)
    qn = [qg[p] * lax.rsqrt(sq[p * tq:(p + 1) * tq] * inv_d + RMS_EPS) * qscale for p in range(ngroups)]
    kn = k * lax.rsqrt(sq[ngroups * tq:] * inv_d + RMS_EPS) * knw_ref[...]
    kall = jnp.concatenate([kprev_ref[...], kn], axis=0)
    vall = jnp.concatenate([vprev_ref[...], v], axis=0)
    kprev_ref[...] = kn[tq - L:]
    vprev_ref[...] = v[tq - L:]
    kroll = pltpu.roll(kall, HEAD_DIM, axis=1)
    vroll = pltpu.roll(vall, HEAD_DIM, axis=1)
    lo_k = _iota2(kall.shape, 1) < HEAD_DIM
    lo_q = _iota2((L, LANE), 1) < HEAD_DIM
    kd = [jnp.where(lo_k, kall, kroll), jnp.where(lo_k, kroll, kall)]
    vdt = [jnp.where(lo_k, vall, vroll).T, jnp.where(lo_k, vroll, vall).T]

    qlane_head = _iota2((1, grp * L), 1) // L

    def block_head(i, j):
        rows = slice(i * L, (i + 1) * L)
        keys = slice(i * L, (i + 2) * L)
        qs = []
        for g in range(grp):
            hq = grp * j + g
            qrow = qn[hq // 2][rows]
            qs.append(jnp.where(lo_q, qrow, 0.0) if hq % 2 == 0 else jnp.where(lo_q, 0.0, qrow))
        bias = bias0_ref[0, j] if i == 0 else bias_ref[0, j]
        sc = _dot_nt(kd[j][keys], jnp.concatenate(qs, axis=0)) + bias
        yield
        sink = jnp.zeros((1, grp * L), F32)
        for g in range(grp):
            sink = jnp.where(qlane_head == g, sink_ref[layer, grp * j + g], sink)
        m = jnp.maximum(jnp.max(sc, axis=0, keepdims=True), sink)
        pe = jnp.exp(sc - m)
        den = jnp.sum(pe, axis=0, keepdims=True) + jnp.exp(sink - m)
        yield
        ot = _dot(vdt[j][:, keys], pe) / den
        yield
        o = ot.T
        for g in range(0, grp, 2):
            o_ref[0, rows, (grp * j + g) // 2 * LANE:((grp * j + g) // 2 + 1) * LANE] = jnp.where(
                lo_q, o[g * L:(g + 1) * L], o[(g + 1) * L:(g + 2) * L]).astype(o_ref.dtype)

    gens = [block_head(i, j) for i in range(nsub) for j in range(ATTN_KV_HEADS)]
    _run_staggered(gens, list(range(len(gens))))


def _swa(zc, bias, qnw, knw, sinks, layer, tq):
    bsz, s, _ = zc.shape
    L = ATTN_BLOCK
    bias_block = (1,) + bias.shape[1:]
    return pl.pallas_call(
        functools.partial(_swa_kernel, nsub=tq // L, layer=layer),
        grid=(bsz, s // tq),
        in_specs=[pl.BlockSpec((1, tq, ATTN_IN), lambda b, n: (b, n, 0)),
                  pl.BlockSpec(bias_block, lambda b, n: (jnp.minimum(n, 1), 0, 0, 0)),
                  pl.BlockSpec(bias_block, lambda b, n: (1, 0, 0, 0)),
                  _layer_param_spec((1, LANE), layer),
                  _layer_param_spec((1, LANE), layer),
                  pl.BlockSpec(memory_space=pltpu.SMEM)],
        out_specs=pl.BlockSpec((1, tq, ATTN_W), lambda b, n: (b, n, 0)),
        out_shape=jax.ShapeDtypeStruct((bsz, s, ATTN_W), BF16),
        scratch_shapes=[pltpu.VMEM((L, ATTN_KV_W), F32), pltpu.VMEM((L, ATTN_KV_W), F32)],
        compiler_params=pltpu.CompilerParams(dimension_semantics=("parallel", "arbitrary"),
                                             vmem_limit_bytes=VMEM_LIMIT),
        name="swa",
    )(zc, bias, bias, qnw, knw, sinks)


def _mlp_kernel(x_ref, ya_ref, yb_ref, yc_ref, wo_ref, ln2_ref, w1_ref, w2_ref, o_ref, *, tf):
    y = (jnp.dot(ya_ref[...], wo_ref[0:MIX_W, :], preferred_element_type=F32)
         + jnp.dot(yb_ref[...], wo_ref[MIX_W:2 * MIX_W, :], preferred_element_type=F32)
         + jnp.dot(yc_ref[...], wo_ref[2 * MIX_W:, :], preferred_element_type=F32))
    x1 = x_ref[...] + y
    h = (x1 * lax.rsqrt(jnp.mean(x1 * x1, axis=-1, keepdims=True) + RMS_EPS) * ln2_ref[...]).astype(BF16)
    acc = x1
    for j in range(D_FF // tf):
        u = jnp.dot(h, w1_ref[:, j * tf:(j + 1) * tf], preferred_element_type=F32)
        u = jnp.square(jnp.maximum(u, 0.0))
        acc = acc + jnp.dot(u.astype(BF16), w2_ref[j * tf:(j + 1) * tf, :], preferred_element_type=F32)
    o_ref[...] = acc


def _layer_weight_spec(shape, layer):
    return pl.BlockSpec((None,) + shape, lambda *_: (layer, 0, 0), pipeline_mode=pl.Buffered(1))


def _layer_param_spec(shape, layer):
    return pl.BlockSpec((None,) + shape, lambda *_: (layer, 0, 0))


def _out_mlp(x2, ya, yb, yc, wo, ln2, w1, w2, layer, tm, tf):
    t = x2.shape[0]
    row = lambda i: (i, 0)
    return pl.pallas_call(
        functools.partial(_mlp_kernel, tf=tf),
        grid=(t // tm,),
        in_specs=[pl.BlockSpec((tm, D_MODEL), row),
                  pl.BlockSpec((tm, MIX_W), row),
                  pl.BlockSpec((tm, MIX_W), row),
                  pl.BlockSpec((tm, ATTN_W), row),
                  _layer_weight_spec((D_MODEL, D_MODEL), layer),
                  _layer_param_spec((1, D_MODEL), layer),
                  _layer_weight_spec((D_MODEL, D_FF), layer),
                  _layer_weight_spec((D_FF, D_MODEL), layer)],
        out_specs=pl.BlockSpec((tm, D_MODEL), row),
        out_shape=jax.ShapeDtypeStruct((t, D_MODEL), F32),
        compiler_params=pltpu.CompilerParams(dimension_semantics=("parallel",),
                                             vmem_limit_bytes=VMEM_LIMIT),
        name="out_mlp",
    )(x2, ya, yb, yc, wo, ln2, w1, w2)


def _pad_rows(m, rows, offset):
    return jnp.zeros((m.shape[0], rows, m.shape[2]), m.dtype).at[:, offset:offset + m.shape[1]].set(m)


def _prepare_params(p):
    depth = p["w_in"].shape[0]
    vec = lambda a: a.reshape(depth, 1, -1)
    per_head = lambda a: jnp.repeat(a, HEAD_DIM, axis=1)[:, None, :]
    tiled = lambda a, n: jnp.tile(a, (1, n))[:, None, :]
    w_in = p["w_in"].astype(BF16)
    o_ba = RWKV_IN + GDN_MAIN
    o_c = RWKV_IN + GDN_IN
    w_re = jnp.concatenate(
        [w_in[:, :, :o_ba], w_in[:, :, o_c:], w_in[:, :, o_ba:o_c],
         jnp.zeros((depth, D_MODEL, LANE - 2 * N_HEADS), w_in.dtype)], axis=2)
    return dict(
        w_in=w_re, w_out=p["w_out"].astype(BF16),
        w_ff1=p["w_ff1"].astype(BF16), w_ff2=p["w_ff2"].astype(BF16),
        ln1_w=vec(p["ln1_w"]), ln2_w=vec(p["ln2_w"]),
        rwkv_mu=vec(p["rwkv_mu"]), rwkv_w0=vec(p["rwkv_w0"]), rwkv_a0=vec(p["rwkv_a0"]),
        rwkv_w_up=_pad_rows(p["rwkv_w_up"], LANE, 0).astype(BF16),
        rwkv_a_up=_pad_rows(p["rwkv_a_up"], LANE, RWKV_DECAY_RANK).astype(BF16),
        rwkv_g_up=p["rwkv_g_up"].astype(BF16),
        rwkv_k_k=vec(p["rwkv_k_k"]), rwkv_k_a=vec(p["rwkv_k_a"]), rwkv_r_k=vec(p["rwkv_r_k"]),
        rwkv_lnx_w=vec(p["rwkv_lnx_w"]), rwkv_lnx_b=vec(p["rwkv_lnx_b"]),
        gdn_conv_w=p["gdn_conv_w"], gdn_a_log=per_head(p["gdn_a_log"]), gdn_dt_bias=per_head(p["gdn_dt_bias"]),
        gdn_norm_w=tiled(p["gdn_norm_w"], N_HEADS),
        attn_q_norm_w=tiled(p["attn_q_norm_w"], LANE // HEAD_DIM),
        attn_k_norm_w=tiled(p["attn_k_norm_w"], LANE // HEAD_DIM),
        attn_sinks=p["attn_sinks"])


def _layer(x2, bsz, s, bias, consts, p, layer):
    t = bsz * s
    tb = min(MIX_TB, s)
    nseq = MIX_NSEQ if bsz % MIX_NSEQ == 0 else 1
    za, zb, zc, zba = _inproj(x2, p["ln1_w"], p["w_in"], layer, min(1024, t))
    ya = _rwkv(za.reshape(bsz, s, RWKV_IN), p["rwkv_mu"], p["rwkv_w0"], p["rwkv_w_up"], p["rwkv_a0"],
               p["rwkv_a_up"], p["rwkv_g_up"], p["rwkv_k_k"], p["rwkv_k_a"], p["rwkv_r_k"],
               p["rwkv_lnx_w"], p["rwkv_lnx_b"], consts, layer, tb, nseq)
    yb = _gdn(zb.reshape(bsz, s, GDN_MAIN), zba.reshape(bsz, s, LANE), p["gdn_conv_w"], p["gdn_a_log"],
              p["gdn_dt_bias"], p["gdn_norm_w"], consts, layer, tb, nseq)
    yc = _swa(zc.reshape(bsz, s, ATTN_IN), bias, p["attn_q_norm_w"], p["attn_k_norm_w"], p["attn_sinks"],
              layer, min(SWA_TQ, s))
    return _out_mlp(x2, ya.reshape(t, MIX_W), yb.reshape(t, MIX_W), yc.reshape(t, ATTN_W),
                    p["w_out"], p["ln2_w"], p["w_ff1"], p["w_ff2"], layer, min(1024, t), min(1024, D_FF))


def kernel(x, ln1_w, w_in, rwkv_mu, rwkv_w0, rwkv_w_up, rwkv_a0, rwkv_a_up, rwkv_g_up, rwkv_k_k, rwkv_k_a,
           rwkv_r_k, rwkv_lnx_w, rwkv_lnx_b, gdn_conv_w, gdn_a_log, gdn_dt_bias, gdn_norm_w, attn_q_norm_w,
           attn_k_norm_w, attn_sinks, rel_bias, w_out, ln2_w, w_ff1, w_ff2):
    stacked = dict(ln1_w=ln1_w, w_in=w_in, rwkv_mu=rwkv_mu, rwkv_w0=rwkv_w0, rwkv_w_up=rwkv_w_up,
                   rwkv_a0=rwkv_a0, rwkv_a_up=rwkv_a_up, rwkv_g_up=rwkv_g_up, rwkv_k_k=rwkv_k_k,
                   rwkv_k_a=rwkv_k_a, rwkv_r_k=rwkv_r_k, rwkv_lnx_w=rwkv_lnx_w, rwkv_lnx_b=rwkv_lnx_b,
                   gdn_conv_w=gdn_conv_w, gdn_a_log=gdn_a_log, gdn_dt_bias=gdn_dt_bias, gdn_norm_w=gdn_norm_w,
                   attn_q_norm_w=attn_q_norm_w, attn_k_norm_w=attn_k_norm_w, attn_sinks=attn_sinks,
                   w_out=w_out, ln2_w=ln2_w, w_ff1=w_ff1, w_ff2=w_ff2)
    bsz, s, _ = x.shape
    bias = _bias_band(rel_bias)
    consts = _mixer_consts()
    params = _prepare_params(stacked)
    x2 = x.reshape(bsz * s, D_MODEL)
    for layer in range(w_in.shape[0]):
        x2 = _layer(x2, bsz, s, bias, consts, params, layer)
    return x2.reshape(bsz, s, D_MODEL)
```

```python
import collections
import functools
import math

import numpy as np
import jax
import jax.numpy as jnp
from jax import lax
from jax.experimental import pallas as pl
from jax.experimental.pallas import tpu as pltpu

F32 = jnp.float32
BF16 = jnp.bfloat16

D_MODEL = 1024
DEPTH = 4
HEAD_DIM = 64
RMS_EPS = 1e-6
L2_EPS = 1e-6
N_HEADS = 4
MIX_W = N_HEADS * HEAD_DIM
RWKV_DECAY_RANK = 64
RWKV_ICLR_RANK = 64
RWKV_GATE_RANK = 128
RWKV_GN_EPS = 64e-5
RWKV_IN = 3 * MIX_W + RWKV_DECAY_RANK + RWKV_ICLR_RANK + RWKV_GATE_RANK
GDN_CONV = 4
GDN_MAIN = 4 * MIX_W
GDN_IN = GDN_MAIN + 2 * N_HEADS
ATTN_Q_HEADS = 8
ATTN_KV_HEADS = 2
ATTN_W = ATTN_Q_HEADS * HEAD_DIM
ATTN_KV_W = ATTN_KV_HEADS * HEAD_DIM
ATTN_IN = ATTN_W + 2 * ATTN_KV_W
ATTN_BLOCK = 128
WINDOW = 128
NUM_BUCKETS = 32
MAX_DISTANCE = 128
D_FF = 4 * D_MODEL
LANE = 128
F32_ROWS = 8
CHUNK = 64
MIX_TB = 256
MIX_NSEQ = 2
MIX_CHUNK_LEAD = 2
MIX_SEQ_LEAD = 1
SWA_TQ = 2048
VMEM_LIMIT = 56 * 1024 * 1024


def _dot(a, b):
    return jnp.dot(a.astype(BF16), b.astype(BF16), preferred_element_type=F32)


def _dot_nt(a, b):
    return lax.dot_general(a.astype(BF16), b.astype(BF16), (((1,), (1,)), ((), ())),
                           preferred_element_type=F32)


def _bf16_parts(x, parts):
    out = []
    r = x
    for p in range(parts):
        hi = r.astype(BF16)
        out.append(hi)
        if p + 1 < parts:
            r = r - hi.astype(F32)
    return out


def _split_dot_r(x, m_bf16, parts):
    rows = x.shape[0]
    t = jnp.dot(jnp.concatenate(_bf16_parts(x, parts), axis=0), m_bf16, preferred_element_type=F32)
    acc = t[:rows]
    for p in range(1, parts):
        acc = acc + t[p * rows:(p + 1) * rows]
    return acc


def _split_dot_l(m_bf16, x, parts):
    cols = x.shape[1]
    t = jnp.dot(m_bf16, jnp.concatenate(_bf16_parts(x, parts), axis=1), preferred_element_type=F32)
    acc = t[:, :cols]
    for p in range(1, parts):
        acc = acc + t[:, p * cols:(p + 1) * cols]
    return acc


def _sigmoid(x):
    return 1.0 / (1.0 + jnp.exp(-x))


def _softplus(x):
    return jnp.maximum(x, 0.0) + jnp.log(1.0 + jnp.exp(-jnp.abs(x)))


def _iota2(shape, dim):
    return lax.broadcasted_iota(jnp.int32, shape, dim)


def _head_ones(n):
    r = _iota2((n, n), 0) // HEAD_DIM
    c = _iota2((n, n), 1) // HEAD_DIM
    return (r == c).astype(BF16)


def _tri_masks():
    t = _iota2((CHUNK, MIX_W), 0)
    s = _iota2((CHUNK, MIX_W), 1) % HEAD_DIM
    return s < t, s <= t


def _run_staggered(gens, starts):
    done = [False] * len(gens)
    rnd = 0
    while not all(done):
        for i, g in enumerate(gens):
            if rnd >= starts[i] and not done[i]:
                try:
                    next(g)
                except StopIteration:
                    done[i] = True
        rnd += 1


MixConsts = collections.namedtuple("MixConsts", "ho ho32 tri cum")


def _mixer_consts():
    n = MIX_W
    r = np.arange(n)[:, None]
    c = np.arange(n)[None, :]
    ho = (r // HEAD_DIM) == (c // HEAD_DIM)
    tri = [r == c, (r // 2) == (c // 2)]
    s = 2
    while s < CHUNK:
        tri.append(((r // (2 * s)) == (c // (2 * s))) & ((r // s) % 2 == 1) & ((c // s) % 2 == 0))
        s *= 2
    cum = np.arange(CHUNK)[None, :] <= np.arange(CHUNK)[:, None]
    return MixConsts(jnp.asarray(ho, BF16), jnp.asarray(ho, F32), jnp.asarray(np.stack(tri), BF16),
                     jnp.asarray(cum, BF16))


def _const_specs():
    fixed = lambda shape: pl.BlockSpec(shape, lambda b, j: (0,) * len(shape), pipeline_mode=pl.Buffered(1))
    ntri = int(math.log2(CHUNK)) + 1
    return [fixed((MIX_W, MIX_W)), fixed((MIX_W, MIX_W)), fixed((ntri, MIX_W, MIX_W)), fixed((CHUNK, CHUNK))]


def _stack_heads(xb, ho):
    return jnp.concatenate([xb] * N_HEADS, axis=0) * ho


def _heads_to_rows(x, hi_half):
    lo = _iota2((x.shape[0], LANE), 1) < HEAD_DIM
    blocks = []
    for h in range(N_HEADS):
        col = x[:, (h // 2) * LANE:(h // 2 + 1) * LANE]
        if (h % 2 == 1) != hi_half:
            col = pltpu.roll(col, HEAD_DIM, axis=1)
        blocks.append(jnp.where(lo != hi_half, col, 0.0))
    return jnp.concatenate(blocks, axis=0)


def _rows_to_heads(z, hi_half):
    c = z.shape[0] // N_HEADS
    lo = _iota2((c, LANE), 1) < HEAD_DIM
    cols = []
    for p in range(N_HEADS // 2):
        even = z[2 * p * c:(2 * p + 1) * c]
        odd = z[(2 * p + 1) * c:(2 * p + 2) * c]
        if hi_half:
            even = pltpu.roll(even, HEAD_DIM, axis=1)
        else:
            odd = pltpu.roll(odd, HEAD_DIM, axis=1)
        cols.append(jnp.where(lo, even, odd))
    return jnp.concatenate(cols, axis=1)


def _tri_inv_stages(a, tri_ref):
    n = a.shape[0]
    t = tri_ref[0] + a * tri_ref[1]
    s = 2
    for lvl in range(2, tri_ref.shape[0]):
        a_off = a * tri_ref[lvl]
        if s % F32_ROWS:
            inner = jnp.dot(a_off, t, preferred_element_type=F32).astype(BF16)
            yield None
            t = t + jnp.dot(t, inner, preferred_element_type=F32).astype(BF16)
        else:
            odd = [slice(r0 + s, r0 + 2 * s) for r0 in range(0, n, 2 * s)]
            zero = jnp.zeros((s, n), BF16)
            pick = lambda x: jnp.concatenate([x[r] for r in odd], axis=0)
            spread = lambda x: jnp.concatenate(
                [blk for i in range(len(odd)) for blk in (zero, x[i * s:(i + 1) * s])], axis=0)
            inner = spread(jnp.dot(pick(a_off), t, preferred_element_type=F32).astype(BF16))
            yield None
            t = t + spread(jnp.dot(pick(t), inner, preferred_element_type=F32).astype(BF16))
        yield None
        s *= 2
    yield t


def _chain_step(carry, qp, y0, w2, u0, upd_extra, upd_lhs_t, decay, ho32):
    h = carry["h"]
    uy = _dot(jnp.concatenate([w2, qp], axis=0), h)
    yield None
    u = u0 + uy[:CHUNK]
    rows = u if upd_extra is None else jnp.concatenate([u, upd_extra], axis=0)
    carry["h"] = h * decay + jnp.dot(upd_lhs_t, rows.astype(BF16), preferred_element_type=F32) * ho32
    yield uy[CHUNK:] + y0


def _shifted_rows(z_ref, prev_ref, q, c):
    if c == 0:
        return jnp.concatenate([prev_ref[q], z_ref[q, 0:CHUNK, :]], axis=0)
    return z_ref[q, c * CHUNK - 8:(c + 1) * CHUNK, :]


def _mixer_body(chunk_fn, refs, tb, nseq, seq_major=False):
    prev_ref, st_ref = refs[-2:]

    @pl.when(pl.program_id(1) == 0)
    def _():
        prev_ref[...] = jnp.zeros_like(prev_ref)
        st_ref[...] = jnp.zeros_like(st_ref)

    assert MIX_CHUNK_LEAD >= 1
    carries = [{} for _ in range(nseq)]
    order = [(c, q) for c in range(tb // CHUNK) for q in range(nseq)]
    if seq_major:
        order.sort(key=lambda cq: cq[1])
    gens = [chunk_fn(q, c, carries[q], *refs, tb) for c, q in order]
    _run_staggered(gens, [c * MIX_CHUNK_LEAD + q * MIX_SEQ_LEAD for c, q in order])


def _inproj_kernel(x_ref, lnw_ref, w_ref, za_ref, zb_ref, zc_ref, zba_ref):
    x = x_ref[...]
    h = x * lax.rsqrt(jnp.mean(x * x, axis=-1, keepdims=True) + RMS_EPS) * lnw_ref[...]
    hb = h.astype(BF16)
    o0 = RWKV_IN
    o1 = o0 + GDN_MAIN
    o2 = o1 + ATTN_IN
    za_ref[...] = jnp.dot(hb, w_ref[:, 0:o0], preferred_element_type=F32)
    zb_ref[...] = jnp.dot(hb, w_ref[:, o0:o1], preferred_element_type=F32)
    zc_ref[...] = jnp.dot(hb, w_ref[:, o1:o2], preferred_element_type=F32)
    zba_ref[...] = jnp.dot(hb, w_ref[:, o2:o2 + LANE], preferred_element_type=F32)


def _inproj(x2, lnw, w, layer, tm):
    t = x2.shape[0]
    n = w.shape[2]
    row = lambda i: (i, 0)
    return pl.pallas_call(
        _inproj_kernel,
        grid=(t // tm,),
        in_specs=[pl.BlockSpec((tm, D_MODEL), row),
                  _layer_param_spec((1, D_MODEL), layer),
                  _layer_weight_spec((D_MODEL, n), layer)],
        out_specs=[pl.BlockSpec((tm, RWKV_IN), row),
                   pl.BlockSpec((tm, GDN_MAIN), row),
                   pl.BlockSpec((tm, ATTN_IN), row),
                   pl.BlockSpec((tm, LANE), row)],
        out_shape=[jax.ShapeDtypeStruct((t, RWKV_IN), F32),
                   jax.ShapeDtypeStruct((t, GDN_MAIN), F32),
                   jax.ShapeDtypeStruct((t, ATTN_IN), F32),
                   jax.ShapeDtypeStruct((t, LANE), F32)],
        compiler_params=pltpu.CompilerParams(dimension_semantics=("parallel",),
                                             vmem_limit_bytes=VMEM_LIMIT),
        name="inproj",
    )(x2, lnw, w)


def _rwkv_chunk(q, c, carry, z_ref, mu_ref, w0_ref, wup_ref, a0_ref, aup_ref, gup_ref, kk_ref, ka_ref, rk_ref,
                lnw_ref, lnb_ref, ho_ref, ho32_ref, tri_ref, cum_ref, o_ref, prev_ref, st_ref, tb):
    w = MIX_W
    nchunk = tb // CHUNK
    ext = _shifted_rows(z_ref, prev_ref, q, c)
    z = ext[8:]
    zprev = pltpu.roll(ext, 1, axis=0)[8:]
    zz = z + (zprev - z) * mu_ref[...]
    r = zz[:, 0:w]
    k = zz[:, w:2 * w]
    v = zz[:, 2 * w:3 * w]
    dwa = zz[:, 3 * w:3 * w + LANE]
    dg = zz[:, 3 * w + LANE:]
    yield
    log_w = -_softplus(-(w0_ref[...] + _dot(jnp.tanh(dwa), wup_ref[...]))) - 0.5
    ld = -jnp.exp(log_w)
    iclr =_sigmoid(a0_ref[...] + _dot(dwa, aup_ref[...]))
    gate = _dot(_sigmoid(dg), gup_ref[...])
    yield
    ho = ho_ref[...]
    kkv = k * kk_ref[...]
    k2 = k * (1.0 + (iclr - 1.0) * ka_ref[...])
    sums = _split_dot_r(jnp.concatenate([kkv * kkv, r * k2 * rk_ref[...]], axis=0), ho, 1)
    kkn = kkv * lax.rsqrt(sums[:CHUNK] + L2_EPS)
    bonus = sums[CHUNK:] * v
    b = kkn * iclr
    yield
    cum = _split_dot_l(cum_ref[...], ld, 2)
    e_neg = jnp.exp(-cum)
    at = (-kkn) * jnp.exp(cum - ld)
    atb = at.astype(BF16)
    bt = (b * e_neg).astype(BF16)
    kt = (k2 * e_neg).astype(BF16)
    rt = r * jnp.exp(cum)
    at_rows =_heads_to_rows(at, False)
    v_rows = _heads_to_rows(v, True).astype(BF16)
    yield
    strict, incl = _tri_masks()
    g = lax.dot_general(jnp.concatenate([atb, rt.astype(BF16)], axis=0),
                        jnp.concatenate([_stack_heads(bt, ho), _stack_heads(kt, ho)], axis=0),
                        (((1,), (1,)), ((), ())), preferred_element_type=F32)
    yield
    a_ab = _stack_heads(jnp.where(strict, g[:CHUNK, :w], 0.0).astype(BF16), ho)
    a_ak = _stack_heads(jnp.where(strict, g[:CHUNK, w:], 0.0).astype(BF16), ho)
    a_rb = _stack_heads(jnp.where(incl, g[CHUNK:, :w], 0.0).astype(BF16), ho)
    a_rk = _stack_heads(jnp.where(incl, g[CHUNK:, w:], 0.0).astype(BF16), ho)
    yield
    t_inv = None
    for t_inv in _tri_inv_stages(a_ab, tri_ref):
        if t_inv is None:
            yield
    pv = jnp.dot(jnp.concatenate([a_ak, a_rk], axis=0), v_rows, preferred_element_type=F32)
    yield
    wu = jnp.dot(t_inv, (at_rows + pv[:w]).astype(BF16), preferred_element_type=F32)
    yield
    qy = jnp.dot(a_rb, wu.astype(BF16), preferred_element_type=F32) + pv[w:]
    yield
    w2 = _rows_to_heads(wu, False)
    u0 = _rows_to_heads(wu, True)
    qp = rt + _rows_to_heads(qy, False)
    y0 = _rows_to_heads(qy, True)
    last = cum[CHUNK - 1:CHUNK, :]
    to_end = jnp.exp(last - cum)
    upd_lhs_t = jnp.concatenate([b * to_end, k2 * to_end], axis=0).T.astype(BF16)
    decay_col = jnp.broadcast_to(jnp.exp(last), (8, w)).T[:, 0:1]
    yield
    if c == 0:
        carry["h"] = st_ref[q]
    y = None
    for y in _chain_step(carry, qp, y0, w2, u0, v, upd_lhs_t, decay_col, ho32_ref[...]):
        if y is None:
            yield
    if c == nchunk - 1:
        st_ref[q] = carry["h"]
        prev_ref[q] = z_ref[q, tb - 8:tb, :]
    yield
    inv_n = 1.0 / HEAD_DIM
    mean = _split_dot_r(y, ho, 1) * inv_n
    d = y - mean
    yield
    var = _split_dot_r(d * d, ho, 1) * inv_n
    yn = d * lax.rsqrt(var + RWKV_GN_EPS) * lnw_ref[...] + lnb_ref[...]
    o_ref[q, c * CHUNK:(c + 1) * CHUNK, :] = ((yn + bonus) * gate).astype(o_ref.dtype)


def _rwkv_kernel(*refs, tb, nseq):
    _mixer_body(_rwkv_chunk, refs, tb, nseq)


def _rwkv(za, mu, w0, wup, a0, aup, gup, k_k, k_a, r_k, lnw, lnb, consts, layer, tb, nseq):
    bsz, s, _ = za.shape
    vec = lambda n: _layer_param_spec((1, n), layer)
    mat = lambda m, n: _layer_param_spec((m, n), layer)
    return pl.pallas_call(
        functools.partial(_rwkv_kernel, tb=tb, nseq=nseq),
        grid=(bsz // nseq, s // tb),
        in_specs=[pl.BlockSpec((nseq, tb, RWKV_IN), lambda b, j: (b, j, 0)),
                  vec(RWKV_IN), vec(MIX_W), mat(LANE, MIX_W), vec(MIX_W), mat(LANE, MIX_W),
                  mat(RWKV_GATE_RANK, MIX_W), vec(MIX_W), vec(MIX_W), vec(MIX_W), vec(MIX_W), vec(MIX_W)]
        + _const_specs(),
        out_specs=pl.BlockSpec((nseq, tb, MIX_W), lambda b, j: (b, j, 0)),
        out_shape=jax.ShapeDtypeStruct((bsz, s, MIX_W), BF16),
        scratch_shapes=[pltpu.VMEM((nseq, 8, RWKV_IN), F32), pltpu.VMEM((nseq, MIX_W, MIX_W), F32)],
        compiler_params=pltpu.CompilerParams(dimension_semantics=("parallel", "arbitrary"),
                                             vmem_limit_bytes=VMEM_LIMIT),
        name="rwkv7",
    )(za, mu, w0, wup, a0, aup, gup, k_k, k_a, r_k, lnw, lnb, *consts)


def _gdn_chunk(q, c, carry, z_ref, ba_ref, cw_ref, alog_ref, dtb_ref, nw_ref, ho_ref, ho32_ref, tri_ref, cum_ref,
               o_ref, prev_ref, st_ref, tb):
    w = MIX_W
    nchunk = tb // CHUNK
    ext = _shifted_rows(z_ref, prev_ref, q, c)
    x = ext[:, :3 * w]
    gate = ext[8:, 3 * w:]
    acc = x[8:] * cw_ref[GDN_CONV - 1:GDN_CONV, :]
    for s in range(1, GDN_CONV):
        acc = acc + pltpu.roll(x, s, axis=0)[8:] * cw_ref[GDN_CONV - 1 - s:GDN_CONV - s, :]
    yield
    qkv = acc * _sigmoid(acc)
    qh = qkv[:, :w]
    k = qkv[:, w:2 * w]
    v = qkv[:, 2 * w:]
    ho = ho_ref[...]
    sums = _split_dot_r(jnp.concatenate([qh * qh, k * k], axis=0), ho, 1)
    qn = qh * lax.rsqrt(sums[:CHUNK] + L2_EPS) * (HEAD_DIM ** -0.5)
    kn = k * lax.rsqrt(sums[CHUNK:] + L2_EPS)
    yield
    ba = ba_ref[q, c * CHUNK:(c + 1) * CHUNK, :]
    er = _iota2((LANE, 2 * w), 0)
    ec = _iota2((LANE, 2 * w), 1)
    spread = (er == (ec % w) // HEAD_DIM + N_HEADS * (ec // w)).astype(BF16)
    ba_full = _split_dot_r(ba, spread, 2)
    beta = _sigmoid(ba_full[:, :w])
    g = -jnp.exp(alog_ref[...]) * _softplus(ba_full[:, w:] + dtb_ref[...])
    yield
    gc = _split_dot_l(cum_ref[...], g, 3)
    gt = gc.T
    eg = jnp.exp(gc)
    kb = kn * beta
    qe = qn * eg
    kv_rows = (_heads_to_rows(kb * eg, False) + _heads_to_rows(v * beta, True)).astype(BF16)
    yield
    strict, incl = _tri_masks()
    s1 = lax.dot_general(jnp.concatenate([kb, qn], axis=0).astype(BF16), _stack_heads(kn.astype(BF16), ho),
                         (((1,), (1,)), ((), ())), preferred_element_type=F32)
    yield
    grow = jnp.concatenate([gt[h * HEAD_DIM:h * HEAD_DIM + 1, :] for h in range(N_HEADS)], axis=1)
    dec = jnp.exp(jnp.where(incl, gc - grow, -jnp.inf))
    lower = _stack_heads(jnp.where(strict, -(s1[:CHUNK] * dec), 0.0).astype(BF16), ho)
    intra = _stack_heads((s1[CHUNK:] * dec).astype(BF16), ho)
    yield
    t_inv = None
    for t_inv in _tri_inv_stages(lower, tri_ref):
        if t_inv is None:
            yield
    wu = jnp.dot(t_inv, kv_rows, preferred_element_type=F32)
    yield
    qy = jnp.dot(intra, wu.astype(BF16), preferred_element_type=F32)
    yield
    w2 = -_rows_to_heads(wu, False)
    u0 = _rows_to_heads(wu, True)
    qp = qe - _rows_to_heads(qy, False)
    y0 = _rows_to_heads(qy, True)
    last = gc[CHUNK - 1:CHUNK, :]
    kdec_t = (kn * jnp.exp(last - gc)).T.astype(BF16)
    yield
    if c == 0:
        carry["h"] = st_ref[q]
    o = None
    for o in _chain_step(carry, qp, y0, w2, u0, None, kdec_t, jnp.exp(last), ho32_ref[...]):
        if o is None:
            yield
    if c == nchunk - 1:
        st_ref[q] = carry["h"]
        prev_ref[q] = z_ref[q, tb - 8:tb, :]
    yield
    ms = _split_dot_r(o * o, ho, 1) * (1.0 / HEAD_DIM)
    on = o * lax.rsqrt(ms + RMS_EPS) * nw_ref[...]
    o_ref[q, c * CHUNK:(c + 1) * CHUNK, :] = (on * (gate * _sigmoid(gate))).astype(o_ref.dtype)


def _gdn_kernel(*refs, tb, nseq):
    _mixer_body(_gdn_chunk, refs, tb, nseq, seq_major=True)


def _gdn(zb, zba, cw, alog, dtb, nw, consts, layer, tb, nseq):
    bsz, s, _ = zb.shape
    vec = lambda n: _layer_param_spec((1, n), layer)
    return pl.pallas_call(
        functools.partial(_gdn_kernel, tb=tb, nseq=nseq),
        grid=(bsz // nseq, s // tb),
        in_specs=[pl.BlockSpec((nseq, tb, GDN_MAIN), lambda b, j: (b, j, 0)),
                  pl.BlockSpec((nseq, tb, LANE), lambda b, j: (b, j, 0)),
                  _layer_param_spec((GDN_CONV, 3 * MIX_W), layer),
                  vec(MIX_W), vec(MIX_W), vec(MIX_W)] + _const_specs(),
        out_specs=pl.BlockSpec((nseq, tb, MIX_W), lambda b, j: (b, j, 0)),
        out_shape=jax.ShapeDtypeStruct((bsz, s, MIX_W), BF16),
        scratch_shapes=[pltpu.VMEM((nseq, 8, GDN_MAIN), F32), pltpu.VMEM((nseq, MIX_W, MIX_W), F32)],
        compiler_params=pltpu.CompilerParams(dimension_semantics=("parallel", "arbitrary"),
                                             vmem_limit_bytes=VMEM_LIMIT),
        name="gdn",
    )(zb, zba, cw, alog, dtb, nw, *consts)


def _t5_bucket_table():
    L = ATTN_BLOCK
    i = np.arange(L)[None, :]
    j = np.arange(2 * L)[:, None]
    dist = np.maximum(i + L - j, 0)
    max_exact = NUM_BUCKETS // 2
    nf = np.maximum(dist, max_exact).astype(np.float32)
    large = max_exact + (np.log(nf / max_exact) / math.log(MAX_DISTANCE / max_exact)
                         * (NUM_BUCKETS - max_exact)).astype(np.int32)
    large = np.minimum(large, NUM_BUCKETS - 1)
    return np.where(dist < max_exact, dist, large).astype(np.int32)


def _bias_kernel(bkt_ref, rel_ref, o_ref):
    L = ATTN_BLOCK
    grp = ATTN_Q_HEADS // ATTN_KV_HEADS
    bkt = bkt_ref[...]
    kj = _iota2(bkt.shape, 0)
    dist = _iota2(bkt.shape, 1) + L - kj
    in_window = (dist >= 0) & (dist < WINDOW)
    for h in range(ATTN_Q_HEADS):
        acc = jnp.zeros(bkt.shape, F32)
        for b in range(NUM_BUCKETS):
            acc = jnp.where(bkt == b, rel_ref[b, h], acc)
        cols = slice((h % grp) * L, (h % grp + 1) * L)
        o_ref[1, h // grp, :, cols] = jnp.where(in_window, acc, -jnp.inf)
        o_ref[0, h // grp, :, cols] = jnp.where(in_window & (kj >= L), acc, -jnp.inf)


def _bias_band(rel_bias):
    L = ATTN_BLOCK
    shape = (2, ATTN_KV_HEADS, 2 * L, (ATTN_Q_HEADS // ATTN_KV_HEADS) * L)
    return pl.pallas_call(
        _bias_kernel,
        in_specs=[pl.BlockSpec((2 * L, L), lambda: (0, 0)),
                  pl.BlockSpec(memory_space=pltpu.SMEM)],
        out_specs=pl.BlockSpec(shape, lambda: (0, 0, 0, 0)),
        out_shape=jax.ShapeDtypeStruct(shape, F32),
        name="t5_bias",
    )(jnp.asarray(_t5_bucket_table()), rel_bias)


def _swa_kernel(z_ref, bias0_ref, bias_ref, qnw_ref, knw_ref, sink_ref, o_ref, kprev_ref, vprev_ref, *, nsub, layer):
    @pl.when(pl.program_id(1) == 0)
    def _():
        kprev_ref[...] = jnp.zeros_like(kprev_ref)
        vprev_ref[...] = jnp.zeros_like(vprev_ref)

    L = ATTN_BLOCK
    tq = nsub * L
    grp = ATTN_Q_HEADS // ATTN_KV_HEADS
    ngroups = ATTN_W // LANE
    z = z_ref[0]
    k = z[:, ATTN_W:ATTN_W + ATTN_KV_W]
    v = z[:, ATTN_W + ATTN_KV_W:]
    qg = [z[:, p * LANE:(p + 1) * LANE] for p in range(ngroups)]
    sq = _split_dot_r(jnp.concatenate([x * x for x in qg] + [k * k], axis=0), _head_ones(LANE), 2)
    inv_d = 1.0 / HEAD_DIM
    qscale = qnw_ref[...] * (HEAD_DIM ** -0.5)
    qn = [qg[p] * lax.rsqrt(sq[p * tq:(p + 1) * tq] * inv_d + RMS_EPS) * qscale for p in range(ngroups)]
    kn = k * lax.rsqrt(sq[ngroups * tq:] * inv_d + RMS_EPS) * knw_ref[...]
    kall = jnp.concatenate([kprev_ref[...], kn], axis=0)
    vall = jnp.concatenate([vprev_ref[...], v], axis=0)
    kprev_ref[...] = kn[tq - L:]
    vprev_ref[...] = v[tq - L:]
    kroll = pltpu.roll(kall, HEAD_DIM, axis=1)
    vroll = pltpu.roll(vall, HEAD_DIM, axis=1)
    lo_k = _iota2(kall.shape, 1) < HEAD_DIM
    lo_q = _iota2((L, LANE), 1) < HEAD_DIM
    kd = [jnp.where(lo_k, kall, kroll), jnp.where(lo_k, kroll, kall)]
    vdt = [jnp.where(lo_k, vall, vroll).T, jnp.where(lo_k, vroll, vall).T]

    qlane_head = _iota2((1, grp * L), 1) // L

    def block_head(i, j):
        rows = slice(i * L, (i + 1) * L)
        keys = slice(i * L, (i + 2) * L)
        qs = []
        for g in range(grp):
            hq = grp * j + g
            qrow = qn[hq // 2][rows]
            qs.append(jnp.where(lo_q, qrow, 0.0) if hq % 2 == 0 else jnp.where(lo_q, 0.0, qrow))
        bias = bias0_ref[0, j] if i == 0 else bias_ref[0, j]
        sc = _dot_nt(kd[j][keys], jnp.concatenate(qs, axis=0)) + bias
        yield
        sink = jnp.zeros((1, grp * L), F32)
        for g in range(grp):
            sink = jnp.where(qlane_head == g, sink_ref[layer, grp * j + g], sink)
        m = jnp.maximum(jnp.max(sc, axis=0, keepdims=True), sink)
        pe = jnp.exp(sc - m)
        den = jnp.sum(pe, axis=0, keepdims=True) + jnp.exp(sink - m)
        yield
        ot = _dot(vdt[j][:, keys], pe) / den
        yield
        o = ot.T
        for g in range(0, grp, 2):
            o_ref[0, rows, (grp * j + g) // 2 * LANE:((grp * j + g) // 2 + 1) * LANE] = jnp.where(
                lo_q, o[g * L:(g + 1) * L], o[(g + 1) * L:(g + 2) * L]).astype(o_ref.dtype)

    gens = [block_head(i, j) for i in range(nsub) for j in range(ATTN_KV_HEADS)]
    _run_staggered(gens, list(range(len(gens))))


def _swa(zc, bias, qnw, knw, sinks, layer, tq):
    bsz, s, _ = zc.shape
    L = ATTN_BLOCK
    bias_block = (1,) + bias.shape[1:]
    return pl.pallas_call(
        functools.partial(_swa_kernel, nsub=tq // L, layer=layer),
        grid=(bsz, s // tq),
        in_specs=[pl.BlockSpec((1, tq, ATTN_IN), lambda b, n: (b, n, 0)),
                  pl.BlockSpec(bias_block, lambda b, n: (jnp.minimum(n, 1), 0, 0, 0)),
                  pl.BlockSpec(bias_block, lambda b, n: (1, 0, 0, 0)),
                  _layer_param_spec((1, LANE), layer),
                  _layer_param_spec((1, LANE), layer),
                  pl.BlockSpec(memory_space=pltpu.SMEM)],
        out_specs=pl.BlockSpec((1, tq, ATTN_W), lambda b, n: (b, n, 0)),
        out_shape=jax.ShapeDtypeStruct((bsz, s, ATTN_W), BF16),
        scratch_shapes=[pltpu.VMEM((L, ATTN_KV_W), F32), pltpu.VMEM((L, ATTN_KV_W), F32)],
        compiler_params=pltpu.CompilerParams(dimension_semantics=("parallel", "arbitrary"),
                                             vmem_limit_bytes=VMEM_LIMIT),
        name="swa",
    )(zc, bias, bias, qnw, knw, sinks)


def _mlp_kernel(x_ref, ya_ref, yb_ref, yc_ref, wo_ref, ln2_ref, w1_ref, w2_ref, o_ref, *, tf):
    y = (jnp.dot(ya_ref[...], wo_ref[0:MIX_W, :], preferred_element_type=F32)
         + jnp.dot(yb_ref[...], wo_ref[MIX_W:2 * MIX_W, :], preferred_element_type=F32)
         + jnp.dot(yc_ref[...], wo_ref[2 * MIX_W:, :], preferred_element_type=F32))
    x1 = x_ref[...] + y
    h = (x1 * lax.rsqrt(jnp.mean(x1 * x1, axis=-1, keepdims=True) + RMS_EPS) * ln2_ref[...]).astype(BF16)
    acc = x1
    for j in range(D_FF // tf):
        u = jnp.dot(h, w1_ref[:, j * tf:(j + 1) * tf], preferred_element_type=F32)
        u = jnp.square(jnp.maximum(u, 0.0))
        acc = acc + jnp.dot(u.astype(BF16), w2_ref[j * tf:(j + 1) * tf, :], preferred_element_type=F32)
    o_ref[...] = acc


def _layer_weight_spec(shape, layer):
    return pl.BlockSpec((None,) + shape, lambda *_: (layer, 0, 0), pipeline_mode=pl.Buffered(1))


def _layer_param_spec(shape, layer):
    return pl.BlockSpec((None,) + shape, lambda *_: (layer, 0, 0))


def _out_mlp(x2, ya, yb, yc, wo, ln2, w1, w2, layer, tm, tf):
    t = x2.shape[0]
    row = lambda i: (i, 0)
    return pl.pallas_call(
        functools.partial(_mlp_kernel, tf=tf),
        grid=(t // tm,),
        in_specs=[pl.BlockSpec((tm, D_MODEL), row),
                  pl.BlockSpec((tm, MIX_W), row),
                  pl.BlockSpec((tm, MIX_W), row),
                  pl.BlockSpec((tm, ATTN_W), row),
                  _layer_weight_spec((D_MODEL, D_MODEL), layer),
                  _layer_param_spec((1, D_MODEL), layer),
                  _layer_weight_spec((D_MODEL, D_FF), layer),
                  _layer_weight_spec((D_FF, D_MODEL), layer)],
        out_specs=pl.BlockSpec((tm, D_MODEL), row),
        out_shape=jax.ShapeDtypeStruct((t, D_MODEL), F32),
        compiler_params=pltpu.CompilerParams(dimension_semantics=("parallel",),
                                             vmem_limit_bytes=VMEM_LIMIT),
        name="out_mlp",
    )(x2, ya, yb, yc, wo, ln2, w1, w2)


def _pad_rows(m, rows, offset):
    return jnp.zeros((m.shape[0], rows, m.shape[2]), m.dtype).at[:, offset:offset + m.shape[1]].set(m)


def _prepare_params(p):
    depth = p["w_in"].shape[0]
    vec = lambda a: a.reshape(depth, 1, -1)
    per_head = lambda a: jnp.repeat(a, HEAD_DIM, axis=1)[:, None, :]
    tiled = lambda a, n: jnp.tile(a, (1, n))[:, None, :]
    w_in = p["w_in"].astype(BF16)
    o_ba = RWKV_IN + GDN_MAIN
    o_c = RWKV_IN + GDN_IN
    w_re = jnp.concatenate(
        [w_in[:, :, :o_ba], w_in[:, :, o_c:], w_in[:, :, o_ba:o_c],
         jnp.zeros((depth, D_MODEL, LANE - 2 * N_HEADS), w_in.dtype)], axis=2)
    return dict(
        w_in=w_re, w_out=p["w_out"].astype(BF16),
        w_ff1=p["w_ff1"].astype(BF16), w_ff2=p["w_ff2"].astype(BF16),
        ln1_w=vec(p["ln1_w"]), ln2_w=vec(p["ln2_w"]),
        rwkv_mu=vec(p["rwkv_mu"]), rwkv_w0=vec(p["rwkv_w0"]), rwkv_a0=vec(p["rwkv_a0"]),
        rwkv_w_up=_pad_rows(p["rwkv_w_up"], LANE, 0).astype(BF16),
        rwkv_a_up=_pad_rows(p["rwkv_a_up"], LANE, RWKV_DECAY_RANK).astype(BF16),
        rwkv_g_up=p["rwkv_g_up"].astype(BF16),
        rwkv_k_k=vec(p["rwkv_k_k"]), rwkv_k_a=vec(p["rwkv_k_a"]), rwkv_r_k=vec(p["rwkv_r_k"]),
        rwkv_lnx_w=vec(p["rwkv_lnx_w"]), rwkv_lnx_b=vec(p["rwkv_lnx_b"]),
        gdn_conv_w=p["gdn_conv_w"], gdn_a_log=per_head(p["gdn_a_log"]), gdn_dt_bias=per_head(p["gdn_dt_bias"]),
        gdn_norm_w=tiled(p["gdn_norm_w"], N_HEADS),
        attn_q_norm_w=tiled(p["attn_q_norm_w"], LANE // HEAD_DIM),
        attn_k_norm_w=tiled(p["attn_k_norm_w"], LANE // HEAD_DIM),
        attn_sinks=p["attn_sinks"])


def _layer(x2, bsz, s, bias, consts, p, layer):
    t = bsz * s
    tb = min(MIX_TB, s)
    nseq = MIX_NSEQ if bsz % MIX_NSEQ == 0 else 1
    za, zb, zc, zba = _inproj(x2, p["ln1_w"], p["w_in"], layer, min(1024, t))
    ya = _rwkv(za.reshape(bsz, s, RWKV_IN), p["rwkv_mu"], p["rwkv_w0"], p["rwkv_w_up"], p["rwkv_a0"],
               p["rwkv_a_up"], p["rwkv_g_up"], p["rwkv_k_k"], p["rwkv_k_a"], p["rwkv_r_k"],
               p["rwkv_lnx_w"], p["rwkv_lnx_b"], consts, layer, tb, nseq)
    yb = _gdn(zb.reshape(bsz, s, GDN_MAIN), zba.reshape(bsz, s, LANE), p["gdn_conv_w"], p["gdn_a_log"],
              p["gdn_dt_bias"], p["gdn_norm_w"], consts, layer, tb, nseq)
    yc = _swa(zc.reshape(bsz, s, ATTN_IN), bias, p["attn_q_norm_w"], p["attn_k_norm_w"], p["attn_sinks"],
              layer, min(SWA_TQ, s))
    return _out_mlp(x2, ya.reshape(t, MIX_W), yb.reshape(t, MIX_W), yc.reshape(t, ATTN_W),
                    p["w_out"], p["ln2_w"], p["w_ff1"], p["w_ff2"], layer, min(1024, t), min(1024, D_FF))


def kernel(x, ln1_w, w_in, rwkv_mu, rwkv_w0, rwkv_w_up, rwkv_a0, rwkv_a_up, rwkv_g_up, rwkv_k_k, rwkv_k_a,
           rwkv_r_k, rwkv_lnx_w, rwkv_lnx_b, gdn_conv_w, gdn_a_log, gdn_dt_bias, gdn_norm_w, attn_q_norm_w,
           attn_k_norm_w, attn_sinks, rel_bias, w_out, ln2_w, w_ff1, w_ff2):
    stacked = dict(ln1_w=ln1_w, w_in=w_in, rwkv_mu=rwkv_mu, rwkv_w0=rwkv_w0, rwkv_w_up=rwkv_w_up,
                   rwkv_a0=rwkv_a0, rwkv_a_up=rwkv_a_up, rwkv_g_up=rwkv_g_up, rwkv_k_k=rwkv_k_k,
                   rwkv_k_a=rwkv_k_a, rwkv_r_k=rwkv_r_k, rwkv_lnx_w=rwkv_lnx_w, rwkv_lnx_b=rwkv_lnx_b,
                   gdn_conv_w=gdn_conv_w, gdn_a_log=gdn_a_log, gdn_dt_bias=gdn_dt_bias, gdn_norm_w=gdn_norm_w,
                   attn_q_norm_w=attn_q_norm_w, attn_k_norm_w=attn_k_norm_w, attn_sinks=attn_sinks,
                   w_out=w_out, ln2_w=ln2_w, w_ff1=w_ff1, w_ff2=w_ff2)
    bsz, s, _ = x.shape
    bias = _bias_band(rel_bias)
    consts = _mixer_consts()
    params = _prepare_params(stacked)
    x2 = x.reshape(bsz * s, D_MODEL)
    for layer in range(w_in.shape[0]):
        x2 = _layer(x2, bsz, s, bias, consts, params, layer)
    return x2.reshape(bsz, s, D_MODEL)
```
